```python
import math
import jax
import jax.numpy as jnp
from jax import lax
import numpy as np

D_MODEL = 2048
BATCH = 2
SEQ = 16384
DEPTH = 2

GRID_W = 64
CTX_LEN = 256
EPS = 1e-6
POOL_W = 512
POOL_GROUPS = 4
POOL_GROUP_W = POOL_W // POOL_GROUPS
POOL_WINDOWS = (2, 4, 8, 16)
N_HEADS = 8
Q_LORA = 512
KV_LORA = 256
QK_NOPE = 128
QK_ROPE = 64
QK_HEAD = QK_NOPE + QK_ROPE
V_HEAD = 128
ROPE_FREQS = QK_ROPE // 4
ROPE_THETA = 10000.0
SOFTMAX_SCALE = QK_HEAD ** -0.5
Q_BLOCK = 128
SG_W = 512
SG_GROUPS = 4
SG_GROUP_W = SG_W // SG_GROUPS
CHUNK = 128
N_BRANCH = 3
D_FF = 5632
CONV_W = 3
IN_SPLITS = (POOL_W, POOL_W + Q_LORA, POOL_W + Q_LORA + KV_LORA, POOL_W + Q_LORA + KV_LORA + QK_ROPE, POOL_W + Q_LORA + KV_LORA + QK_ROPE + 2 * SG_W)
IN_COLS = IN_SPLITS[-1] + N_BRANCH * D_MODEL

kernel_name = 'hybrid_pool_mla_sgmlp_convffn_dit'


def rmsnorm(x, g):
    x32 = x.astype(jnp.float32)
    y = x32 * lax.rsqrt(jnp.mean(x32 * x32, axis=-1, keepdims=True) + EPS)
    return (y * g.astype(jnp.float32)).astype(x.dtype)


def layernorm_gain(x, g):
    x32 = x.astype(jnp.float32)
    xc = x32 - jnp.mean(x32, axis=-1, keepdims=True)
    y = xc * lax.rsqrt(jnp.mean(xc * xc, axis=-1, keepdims=True) + EPS)
    return (y * g.astype(jnp.float32)).astype(x.dtype)


def modulate(h, shift, scale):
    return h * (1.0 + scale) + shift


def ada_modulation(s, ada_w, ada_b):
    m = s @ ada_w + ada_b
    return jnp.split(m[:, None, :], 6, axis=-1)


def axial_rope_tables(n_tokens):
    rows = n_tokens // GRID_W
    row = jnp.repeat(jnp.arange(rows, dtype=jnp.float32), GRID_W)
    col = jnp.tile(jnp.arange(GRID_W, dtype=jnp.float32), rows)
    inv_freq = ROPE_THETA ** (-jnp.arange(ROPE_FREQS, dtype=jnp.float32) / ROPE_FREQS)
    ang = jnp.stack([row[:, None] * inv_freq, col[:, None] * inv_freq], axis=1)
    return jnp.cos(ang)[:, None], jnp.sin(ang)[:, None]


def apply_axial_rope(x, cos, sin):
    xr = x.astype(jnp.float32).reshape(x.shape[:-1] + (2, 2, ROPE_FREQS))
    x1, x2 = xr[..., 0, :], xr[..., 1, :]
    out = jnp.stack([x1 * cos - x2 * sin, x2 * cos + x1 * sin], axis=-2)
    return out.reshape(x.shape).astype(x.dtype)


def pool_mix(z, pool_w, pool_scale):
    B, L, _ = z.shape
    cs = jnp.concatenate([jnp.zeros((B, 1, POOL_W), jnp.float32), jnp.cumsum(z.astype(jnp.float32), axis=1)], axis=1)
    t = jnp.arange(L)
    means = []
    for g, w in enumerate(POOL_WINDOWS):
        lo = jnp.clip(t - w // 2, 0, L)
        hi = jnp.clip(t + w // 2, 0, L)
        cg = cs[..., g * POOL_GROUP_W:(g + 1) * POOL_GROUP_W]
        cnt = (hi - lo).astype(jnp.float32)[None, :, None]
        means.append((jnp.take(cg, hi, axis=1) - jnp.take(cg, lo, axis=1)) / cnt)
    pooled = jnp.concatenate(means, axis=-1).astype(z.dtype) - z
    y = jnp.einsum('blgc,gcd->blgd', pooled.reshape(B, L, POOL_GROUPS, POOL_GROUP_W), pool_w)
    return y.reshape(B, L, POOL_W) * pool_scale


def spatial_gating(z, sg_norm_g, sg_w, sg_b):
    B, L, _ = z.shape
    u, v = jnp.split(jax.nn.gelu(z), 2, axis=-1)
    v = layernorm_gain(v, sg_norm_g).reshape(B, L // CHUNK, CHUNK, SG_GROUPS, SG_GROUP_W)
    s = jnp.einsum('gpq,bnqgc->bnpgc', sg_w, v) + sg_b.T[None, None, :, :, None]
    return u * s.reshape(B, L, SG_W)


def mla_queries(z_q, p, rope):
    B, L, _ = z_q.shape
    q = (rmsnorm(z_q, p['q_lat_g']) @ p['w_uq']).reshape(B, L, N_HEADS, QK_HEAD)
    q = rmsnorm(q, p['q_norm_g'])
    if rope is not None:
        q = jnp.concatenate([q[..., :QK_NOPE], apply_axial_rope(q[..., QK_NOPE:], *rope)], axis=-1)
    return q


def mla_keys_values(z_kv, z_kr, p, rope):
    B, L, _ = z_kv.shape
    kv = (rmsnorm(z_kv, p['kv_lat_g']) @ p['w_ukv']).reshape(B, L, N_HEADS, QK_NOPE + V_HEAD)
    k_nope, v = kv[..., :QK_NOPE], kv[..., QK_NOPE:]
    k_rope = jnp.broadcast_to(z_kr[:, :, None, :], (B, L, N_HEADS, QK_ROPE))
    k = rmsnorm(jnp.concatenate([k_nope, k_rope], axis=-1), p['k_norm_g'])
    if rope is not None:
        k = jnp.concatenate([k[..., :QK_NOPE], apply_axial_rope(k[..., QK_NOPE:], *rope)], axis=-1)
    return k, v


def attend_latent(q, k, v):
    B, L, H, Dq = q.shape
    qb = q.reshape(B, L // Q_BLOCK, Q_BLOCK, H, Dq).transpose(1, 0, 2, 3, 4)

    def block(qi):
        s = jnp.einsum('bqhd,bkhd->bhqk', qi, k, preferred_element_type=jnp.float32) * SOFTMAX_SCALE
        pr = jax.nn.softmax(s, axis=-1).astype(v.dtype)
        return jnp.einsum('bhqk,bkhd->bqhd', pr, v)

    o = lax.map(block, qb)
    return o.transpose(1, 0, 2, 3, 4).reshape(B, L, H * V_HEAD)


def attend_context(q, k, v):
    B, C, H, _ = q.shape
    s = jnp.einsum('bqhd,bkhd->bhqk', q, k, preferred_element_type=jnp.float32) * SOFTMAX_SCALE
    pr = jax.nn.softmax(s, axis=-1).astype(v.dtype)
    return jnp.einsum('bhqk,bkhd->bqhd', pr, v).reshape(B, C, H * V_HEAD)


def conv_ffn(h, p):
    L = h.shape[1]
    u = h @ p['ffn_up']
    pad = CONV_W // 2
    up = jnp.pad(u, ((0, 0), (pad, pad), (0, 0)))
    w = p['ffn_conv_w']
    acc = p['ffn_conv_b'] + up[:, 0:L] * w[0]
    for j in range(1, CONV_W):
        acc = acc + up[:, j:j + L] * w[j]
    a, val = jnp.split(acc, 2, axis=-1)
    return (jax.nn.silu(a) * val) @ p['ffn_down']


def merge_branches(z, attn, p):
    z_pool, z_sg, z_gate = z[0], z[4], z[5]
    pool_out = pool_mix(z_pool, p['pool_w'], p['pool_scale'])
    sg_out = spatial_gating(z_sg, p['sg_norm_g'], p['sg_w'], p['sg_b'])
    g_pool, g_mla, g_sg = jnp.split(jax.nn.sigmoid(z_gate), N_BRANCH, axis=-1)
    y = g_pool * (pool_out @ p['w_br_pool']) + g_mla * (attn @ p['w_br_mla']) + g_sg * (sg_out @ p['w_br_sg'])
    return y @ p['w_o']


def hybrid_layer(xl, xc, s_lat, s_ctx, rope, p, ctx_out):
    sh1, sc1, g1, sh2, sc2, g2 = ada_modulation(s_lat, p['ada_w'], p['ada_b'])
    csh1, csc1, cg1, csh2, csc2, cg2 = ada_modulation(s_ctx, p['ada_w'], p['ada_b'])
    hl = modulate(rmsnorm(xl, p['norm1_g']), sh1, sc1)
    hc = modulate(rmsnorm(xc, p['norm1_g']), csh1, csc1)
    zl = jnp.split(hl @ p['w_in'], IN_SPLITS, axis=-1)
    zc = jnp.split(hc @ p['w_in'], IN_SPLITS, axis=-1)
    kc, vc = mla_keys_values(zc[2], zc[3], p, None)
    kl, vl = mla_keys_values(zl[2], zl[3], p, rope)
    ql = mla_queries(zl[1], p, rope)
    attn_l = attend_latent(ql, jnp.concatenate([kl, kc], axis=1), jnp.concatenate([vl, vc], axis=1))
    xl = xl + g1 * merge_branches(zl, attn_l, p)
    xl = xl + g2 * conv_ffn(modulate(rmsnorm(xl, p['norm2_g']), sh2, sc2), p)
    if ctx_out:
        qc = mla_queries(zc[1], p, None)
        attn_c = attend_context(qc, kc, vc)
        xc = xc + cg1 * merge_branches(zc, attn_c, p)
        xc = xc + cg2 * conv_ffn(modulate(rmsnorm(xc, p['norm2_g']), csh2, csc2), p)
    return xl, xc


def setup_inputs(seed: int = 0) -> dict:
    key = jax.random.key(seed)
    ks = jax.random.split(key, 32)
    f32 = jnp.float32

    def nrm(k, shape, scale=1.0):
        return jax.random.normal(k, shape, f32) * scale

    def gain(k, shape, centre=1.0):
        return centre + 0.1 * jax.random.normal(k, shape, f32)

    F2 = 2 * D_FF
    return {
        'x': nrm(ks[0], (BATCH, SEQ, D_MODEL)),
        'c': nrm(ks[1], (BATCH, D_MODEL)),
        'ctx': nrm(ks[2], (BATCH, CTX_LEN, D_MODEL)),
        'c_ctx': nrm(ks[3], (D_MODEL,)),
        'ada_w': nrm(ks[4], (DEPTH, D_MODEL, 6 * D_MODEL), 0.5 * D_MODEL ** -0.5),
        'ada_b': nrm(ks[5], (DEPTH, 6 * D_MODEL), 0.01),
        'norm1_g': gain(ks[6], (DEPTH, D_MODEL)),
        'w_in': nrm(ks[7], (DEPTH, D_MODEL, IN_COLS), D_MODEL ** -0.5),
        'pool_w': nrm(ks[8], (DEPTH, POOL_GROUPS, POOL_GROUP_W, POOL_GROUP_W), POOL_GROUP_W ** -0.5),
        'pool_scale': gain(ks[9], (DEPTH, POOL_W)),
        'q_lat_g': gain(ks[10], (DEPTH, Q_LORA)),
        'w_uq': nrm(ks[11], (DEPTH, Q_LORA, N_HEADS * QK_HEAD), Q_LORA ** -0.5),
        'kv_lat_g': gain(ks[12], (DEPTH, KV_LORA)),
        'w_ukv': nrm(ks[13], (DEPTH, KV_LORA, N_HEADS * (QK_NOPE + V_HEAD)), KV_LORA ** -0.5),
        'q_norm_g': gain(ks[14], (DEPTH, QK_HEAD), 1.5),
        'k_norm_g': gain(ks[15], (DEPTH, QK_HEAD), 1.5),
        'sg_norm_g': gain(ks[16], (DEPTH, SG_W)),
        'sg_w': nrm(ks[17], (DEPTH, SG_GROUPS, CHUNK, CHUNK), CHUNK ** -0.5),
        'sg_b': gain(ks[18], (DEPTH, SG_GROUPS, CHUNK)),
        'w_br_pool': nrm(ks[19], (DEPTH, POOL_W, D_MODEL), POOL_W ** -0.5),
        'w_br_mla': nrm(ks[20], (DEPTH, N_HEADS * V_HEAD, D_MODEL), (N_HEADS * V_HEAD) ** -0.5),
        'w_br_sg': nrm(ks[21], (DEPTH, SG_W, D_MODEL), SG_W ** -0.5),
        'w_o': nrm(ks[22], (DEPTH, D_MODEL, D_MODEL), D_MODEL ** -0.5),
        'norm2_g': gain(ks[23], (DEPTH, D_MODEL)),
        'ffn_up': nrm(ks[24], (DEPTH, D_MODEL, F2), D_MODEL ** -0.5),
        'ffn_conv_w': nrm(ks[25], (DEPTH, CONV_W, F2), 0.3).at[:, CONV_W // 2].add(1.0),
        'ffn_conv_b': nrm(ks[26], (DEPTH, F2), 0.01),
        'ffn_down': nrm(ks[27], (DEPTH, D_FF, D_MODEL), D_FF ** -0.5),
    }


def reference(x, c, ctx, c_ctx, ada_w, ada_b, norm1_g, w_in, pool_w, pool_scale, q_lat_g, w_uq, kv_lat_g, w_ukv, q_norm_g, k_norm_g, sg_norm_g, sg_w, sg_b, w_br_pool, w_br_mla, w_br_sg, w_o, norm2_g, ffn_up, ffn_conv_w, ffn_conv_b, ffn_down):
    s_lat = jax.nn.silu(c)
    s_ctx = jax.nn.silu(c_ctx)[None, :]
    rope = axial_rope_tables(x.shape[1])
    xl, xc = x, ctx
    for i in range(DEPTH):
        p = {
            'ada_w': ada_w[i], 'ada_b': ada_b[i], 'norm1_g': norm1_g[i], 'w_in': w_in[i],
            'pool_w': pool_w[i], 'pool_scale': pool_scale[i],
            'q_lat_g': q_lat_g[i], 'w_uq': w_uq[i], 'kv_lat_g': kv_lat_g[i], 'w_ukv': w_ukv[i],
            'q_norm_g': q_norm_g[i], 'k_norm_g': k_norm_g[i],
            'sg_norm_g': sg_norm_g[i], 'sg_w': sg_w[i], 'sg_b': sg_b[i],
            'w_br_pool': w_br_pool[i], 'w_br_mla': w_br_mla[i], 'w_br_sg': w_br_sg[i], 'w_o': w_o[i],
            'norm2_g': norm2_g[i], 'ffn_up': ffn_up[i], 'ffn_conv_w': ffn_conv_w[i],
            'ffn_conv_b': ffn_conv_b[i], 'ffn_down': ffn_down[i],
        }
        xl, xc = hybrid_layer(xl, xc, s_lat, s_ctx, rope, p, i < DEPTH - 1)
    return xl
```

```python
import functools
import math

import jax
import jax.numpy as jnp
import numpy as np
from jax import lax
from jax.experimental import pallas as pl
from jax.experimental.pallas import tpu as pltpu

F32 = jnp.float32
BF16 = jnp.bfloat16

GRID_W = 64
EPS = 1e-6
POOL_W = 512
POOL_WINDOWS = (2, 4, 8, 16)
POOL_GROUP_W = POOL_W // len(POOL_WINDOWS)
N_HEADS = 8
Q_LORA = 512
KV_LORA = 256
QK_NOPE = 128
QK_ROPE = 64
QK_HEAD = QK_NOPE + QK_ROPE
V_HEAD = 128
ROPE_FREQS = QK_ROPE // 4
ROPE_THETA = 10000.0
SG_W = 512
SG_GROUPS = 4
SG_GROUP_W = SG_W // SG_GROUPS
CHUNK = 128
N_BRANCH = 3
CONV_W = 3

LANES = 128
SUBLANES = 8
BF16_ROWS = 16
HEAD_PAD = 2 * LANES
VMEM_LIMIT_BYTES = 56 * 1024 * 1024

ZS_POOL = 0
ZS_Q = 512
ZS_SG = 1024
ZS_KV = 2048
ZS_COLS = 2560
ROPE_SWAP = np.concatenate([np.arange(16, 32), np.arange(0, 16), np.arange(48, 64), np.arange(32, 48)])


def _cparams(*sem):
    return pltpu.CompilerParams(dimension_semantics=sem, vmem_limit_bytes=VMEM_LIMIT_BYTES)


def _rms_scale(x, width):
    return lax.rsqrt(jnp.sum(x * x, axis=-1, keepdims=True) * (1.0 / width) + EPS)


def _ada_kernel(c_ref, w_ref, b_ref, o_ref):
    c = c_ref[...]
    s = c * jax.nn.sigmoid(c)
    o_ref[...] = jnp.dot(s, w_ref[...], preferred_element_type=F32, precision=lax.Precision.HIGHEST) + b_ref[...]


def _ada(cvec, ada_w, ada_b):
    depth, d, n = ada_w.shape
    tn = 1024
    return pl.pallas_call(
        _ada_kernel,
        grid=(depth, n // tn),
        in_specs=[
            pl.BlockSpec((SUBLANES, d), lambda l, j: (0, 0)),
            pl.BlockSpec((None, d, tn), lambda l, j: (l, 0, j)),
            pl.BlockSpec((None, 1, tn), lambda l, j: (l, 0, j)),
        ],
        out_specs=pl.BlockSpec((None, SUBLANES, tn), lambda l, j: (l, 0, j)),
        out_shape=jax.ShapeDtypeStruct((depth, SUBLANES, n), F32),
        compiler_params=_cparams("parallel", "arbitrary"),
        name="ada_modulation",
    )(cvec, ada_w, ada_b.reshape(depth, 1, n))


def _norm_mod(x, g, mod_ref, shift_row, scale_row):
    y = x * _rms_scale(x, x.shape[-1]) * g
    return y * (1.0 + mod_ref[scale_row:scale_row + 1, :]) + mod_ref[shift_row:shift_row + 1, :]


def _nm_matmul_kernel(x_ref, g_ref, mod_ref, w_ref, o_ref, h_ref, *, sigmoid):
    @pl.when(pl.program_id(2) == 0)
    def _():
        h_ref[...] = _norm_mod(x_ref[...], g_ref[...], mod_ref, 0, 1).astype(BF16)

    z = jnp.dot(h_ref[...], w_ref[...], preferred_element_type=F32)
    if sigmoid:
        z = jax.nn.sigmoid(z)
    o_ref[...] = z.astype(o_ref.dtype)


def _nm_matmul(x, g, mod, w, *, sigmoid, out_dtype, tm, tn):
    b, l, d = x.shape
    n = w.shape[1]
    tm = min(tm, l)
    return pl.pallas_call(
        functools.partial(_nm_matmul_kernel, sigmoid=sigmoid),
        grid=(b, l // tm, n // tn),
        in_specs=[
            pl.BlockSpec((None, tm, d), lambda bi, i, j: (bi, i, 0)),
            pl.BlockSpec((1, d), lambda bi, i, j: (0, 0)),
            pl.BlockSpec((None, 6, d), lambda bi, i, j: (bi, 0, 0)),
            pl.BlockSpec((d, tn), lambda bi, i, j: (0, j)),
        ],
        out_specs=pl.BlockSpec((None, tm, tn), lambda bi, i, j: (bi, i, j)),
        out_shape=jax.ShapeDtypeStruct((b, l, n), out_dtype),
        scratch_shapes=[pltpu.VMEM((tm, d), BF16)],
        compiler_params=_cparams("parallel", "parallel", "arbitrary"),
        name="in_proj_gate" if sigmoid else "in_proj_small",
    )(x, g.reshape(1, d), mod, w)


def _pool_kernel(prev_ref, cur_ref, next_ref, pw_ref, ps_ref, o_ref, ext_ref, *, tm, nt, seq_len):
    ti = pl.program_id(1)
    cur = cur_ref[...]
    ext_ref[SUBLANES:SUBLANES + tm, :] = cur
    ext_ref[0:SUBLANES, :] = jnp.where(ti > 0, prev_ref[...], 0.0)
    ext_ref[SUBLANES + tm:, :] = jnp.where(ti < nt - 1, next_ref[...], 0.0)
    t = ti * tm + lax.broadcasted_iota(jnp.int32, (tm, 1), 0)
    for g, w in enumerate(POOL_WINDOWS):
        r = w // 2
        c0 = g * POOL_GROUP_W
        s = ext_ref[SUBLANES - r:SUBLANES - r + tm, c0:c0 + POOL_GROUP_W]
        for dlt in range(-r + 1, r):
            s = s + ext_ref[SUBLANES + dlt:SUBLANES + dlt + tm, c0:c0 + POOL_GROUP_W]
        cnt = (jnp.minimum(t + r, seq_len) - jnp.maximum(t - r, 0)).astype(F32)
        pooled = s / cnt - cur[:, c0:c0 + POOL_GROUP_W]
        y = jnp.dot(pooled.astype(BF16), pw_ref[g], preferred_element_type=F32)
        o_ref[:, c0:c0 + POOL_GROUP_W] = (y * ps_ref[:, c0:c0 + POOL_GROUP_W]).astype(o_ref.dtype)


def _pool_mix(zs, pool_w, pool_scale, *, tm):
    b, l, _ = zs.shape
    tm = min(tm, l)
    nt = l // tm
    hb = tm // SUBLANES
    return pl.pallas_call(
        functools.partial(_pool_kernel, tm=tm, nt=nt, seq_len=l),
        grid=(b, nt),
        in_specs=[
            pl.BlockSpec((None, SUBLANES, POOL_W), lambda bi, i: (bi, jnp.maximum(i * hb - 1, 0), 0)),
            pl.BlockSpec((None, tm, POOL_W), lambda bi, i: (bi, i, 0)),
            pl.BlockSpec((None, SUBLANES, POOL_W), lambda bi, i: (bi, jnp.minimum((i + 1) * hb, l // SUBLANES - 1), 0)),
            pl.BlockSpec((len(POOL_WINDOWS), POOL_GROUP_W, POOL_GROUP_W), lambda bi, i: (0, 0, 0)),
            pl.BlockSpec((1, POOL_W), lambda bi, i: (0, 0)),
        ],
        out_specs=pl.BlockSpec((None, tm, POOL_W), lambda bi, i: (bi, i, 0)),
        out_shape=jax.ShapeDtypeStruct((b, l, POOL_W), BF16),
        scratch_shapes=[pltpu.VMEM((tm + 2 * SUBLANES, POOL_W), F32)],
        compiler_params=_cparams("parallel", "parallel"),
        name="pool_mix",
    )(zs, zs, zs, pool_w, pool_scale.reshape(1, POOL_W))


def _sg_kernel(z_ref, g_ref, w_ref, b_ref, o_ref, *, tm):
    z = z_ref[...]
    a = z * (0.5 * (1.0 + jnp.tanh(math.sqrt(2.0 / math.pi) * (z + 0.044715 * (z * z * z)))))
    u = a[:, :SG_W]
    v = a[:, SG_W:]
    vc = v - jnp.mean(v, axis=-1, keepdims=True)
    vn = (vc * lax.rsqrt(jnp.mean(vc * vc, axis=-1, keepdims=True) + EPS) * g_ref[...]).astype(BF16)
    for c in range(tm // CHUNK):
        rows = slice(c * CHUNK, (c + 1) * CHUNK)
        for g in range(SG_GROUPS):
            cols = slice(g * SG_GROUP_W, (g + 1) * SG_GROUP_W)
            s = jnp.dot(w_ref[g], vn[rows, cols], preferred_element_type=F32) + b_ref[g]
            o_ref[rows, cols] = (u[rows, cols] * s).astype(o_ref.dtype)


def _spatial_gating(zs, sg_norm_g, sg_w, sg_b, *, tm):
    b, l, _ = zs.shape
    tm = min(tm, l)
    return pl.pallas_call(
        functools.partial(_sg_kernel, tm=tm),
        grid=(b, l // tm),
        in_specs=[
            pl.BlockSpec((None, tm, 2 * SG_W), lambda bi, i: (bi, i, ZS_SG // (2 * SG_W))),
            pl.BlockSpec((1, SG_W), lambda bi, i: (0, 0)),
            pl.BlockSpec((SG_GROUPS, CHUNK, CHUNK), lambda bi, i: (0, 0, 0)),
            pl.BlockSpec((SG_GROUPS, CHUNK, 1), lambda bi, i: (0, 0, 0)),
        ],
        out_specs=pl.BlockSpec((None, tm, SG_W), lambda bi, i: (bi, i, 0)),
        out_shape=jax.ShapeDtypeStruct((b, l, SG_W), BF16),
        compiler_params=_cparams("parallel", "parallel"),
        name="spatial_gating",
    )(zs, sg_norm_g.reshape(1, SG_W), sg_w, sg_b.reshape(SG_GROUPS, CHUNK, 1))


def _q_kernel(z_ref, gl_ref, w_ref, gn_ref, gr_ref, grs_ref, cos_ref, sin_ref, o_ref, *, out_scale):
    z = z_ref[...]
    zn = (z * _rms_scale(z, Q_LORA) * gl_ref[...]).astype(BF16)
    q = jnp.dot(zn, w_ref[...], preferred_element_type=F32)
    nope_w = N_HEADS * QK_NOPE
    rope_w = N_HEADS * QK_ROPE
    lo = lax.broadcasted_iota(jnp.int32, (1, LANES), 1) < QK_ROPE
    cos = cos_ref[...]
    sin = sin_ref[...]
    gn = gn_ref[...] * out_scale
    for pair in range(N_HEADS // 2):
        blk = slice(pair * LANES, (pair + 1) * LANES)
        r2 = q[:, nope_w + pair * LANES:nope_w + (pair + 1) * LANES]
        r2s = q[:, nope_w + rope_w + pair * LANES:nope_w + rope_w + (pair + 1) * LANES]
        sq = r2 * r2
        ss_lo = jnp.sum(jnp.where(lo, sq, 0.0), axis=-1, keepdims=True)
        ss_hi = jnp.sum(jnp.where(lo, 0.0, sq), axis=-1, keepdims=True)
        rot = r2 * gr_ref[:, blk] * cos + r2s * grs_ref[:, blk] * sin
        for half, ss_r in enumerate((ss_lo, ss_hi)):
            h = 2 * pair + half
            nope = q[:, h * QK_NOPE:(h + 1) * QK_NOPE]
            inv = lax.rsqrt((jnp.sum(nope * nope, axis=-1, keepdims=True) + ss_r) * (1.0 / QK_HEAD) + EPS)
            o_ref[h, :, 0:LANES] = (nope * inv * gn).astype(o_ref.dtype)
            keep = lo if half == 0 else jnp.logical_not(lo)
            o_ref[h, :, LANES:HEAD_PAD] = jnp.where(keep, rot * (inv * out_scale), 0.0).astype(o_ref.dtype)


def _q_proj(zs, q_lat_g, w_q, gn, gr, grs, cos_tab, sin_tab, *, out_scale, tm):
    b, l, _ = zs.shape
    tm = min(tm, l)
    return pl.pallas_call(
        functools.partial(_q_kernel, out_scale=out_scale),
        grid=(b, l // tm),
        in_specs=[
            pl.BlockSpec((None, tm, Q_LORA), lambda bi, i: (bi, i, ZS_Q // Q_LORA)),
            pl.BlockSpec((1, Q_LORA), lambda bi, i: (0, 0)),
            pl.BlockSpec(w_q.shape, lambda bi, i: (0, 0)),
            pl.BlockSpec((1, QK_NOPE), lambda bi, i: (0, 0)),
            pl.BlockSpec((1, N_HEADS * QK_ROPE), lambda bi, i: (0, 0)),
            pl.BlockSpec((1, N_HEADS * QK_ROPE), lambda bi, i: (0, 0)),
            pl.BlockSpec((tm, LANES), lambda bi, i: (i, 0)),
            pl.BlockSpec((tm, LANES), lambda bi, i: (i, 0)),
        ],
        out_specs=pl.BlockSpec((None, N_HEADS, tm, HEAD_PAD), lambda bi, i: (bi, 0, i, 0)),
        out_shape=jax.ShapeDtypeStruct((b, N_HEADS, l, HEAD_PAD), BF16),
        compiler_params=_cparams("parallel", "parallel"),
        name="q_proj",
    )(zs, q_lat_g.reshape(1, Q_LORA), w_q, gn, gr, grs, cos_tab, sin_tab)


def _kv_kernel(z_ref, gl_ref, w_ref, gn_ref, gt_ref, tab_ref, k_ref, v_ref):
    z = z_ref[...]
    zkv = z[:, :KV_LORA]
    zn = (zkv * _rms_scale(zkv, KV_LORA) * gl_ref[...]).astype(BF16)
    kv = jnp.dot(zn, w_ref[...], preferred_element_type=F32)
    lo = lax.broadcasted_iota(jnp.int32, (1, LANES), 1) < QK_ROPE
    kr2 = z[:, KV_LORA:KV_LORA + LANES]
    ss_r = jnp.sum(jnp.where(lo, kr2 * kr2, 0.0), axis=-1, keepdims=True)
    t = kr2 * gt_ref[...] * tab_ref[...]
    rot = t + pltpu.roll(t, QK_ROPE, axis=1)
    gn = gn_ref[...]
    ones = jnp.ones((z.shape[0], LANES), v_ref.dtype)
    for h in range(N_HEADS):
        nope = kv[:, h * QK_NOPE:(h + 1) * QK_NOPE]
        inv = lax.rsqrt((jnp.sum(nope * nope, axis=-1, keepdims=True) + ss_r) * (1.0 / QK_HEAD) + EPS)
        k_ref[h, :, 0:LANES] = (nope * inv * gn).astype(k_ref.dtype)
        keep = lo if h % 2 == 0 else jnp.logical_not(lo)
        k_ref[h, :, LANES:HEAD_PAD] = jnp.where(keep, rot * inv, 0.0).astype(k_ref.dtype)
        v_ref[h, :, 0:LANES] = kv[:, N_HEADS * QK_NOPE + h * V_HEAD:N_HEADS * QK_NOPE + (h + 1) * V_HEAD].astype(
            v_ref.dtype)
        v_ref[h, :, LANES:HEAD_PAD] = ones


def _kv_proj(zs, kv_lat_g, w_kv, gn, gt, tab, *, tm):
    b, l, _ = zs.shape
    tm = min(tm, l)
    blk = KV_LORA + 2 * LANES
    shape = jax.ShapeDtypeStruct((b, N_HEADS, l, HEAD_PAD), BF16)
    spec = pl.BlockSpec((None, N_HEADS, tm, HEAD_PAD), lambda bi, i: (bi, 0, i, 0))
    return pl.pallas_call(
        _kv_kernel,
        grid=(b, l // tm),
        in_specs=[
            pl.BlockSpec((None, tm, blk), lambda bi, i: (bi, i, ZS_KV // blk)),
            pl.BlockSpec((1, KV_LORA), lambda bi, i: (0, 0)),
            pl.BlockSpec(w_kv.shape, lambda bi, i: (0, 0)),
            pl.BlockSpec((1, QK_NOPE), lambda bi, i: (0, 0)),
            pl.BlockSpec((1, LANES), lambda bi, i: (0, 0)),
            pl.BlockSpec((tm, LANES), lambda bi, i: (i, 0)),
        ],
        out_specs=[spec, spec],
        out_shape=[shape, shape],
        compiler_params=_cparams("parallel", "parallel"),
        name="kv_proj",
    )(zs, kv_lat_g.reshape(1, KV_LORA), w_kv, gn, gt, tab)


def _attn_kernel(q_ref, *refs, chunks):
    n_src = len(chunks)
    o_ref, acc_ref, m_ref = refs[2 * n_src:]
    q = q_ref[...]
    acc_ref[...] = jnp.zeros_like(acc_ref)
    m_ref[...] = jnp.full_like(m_ref, -jnp.inf)

    def step(k, v):
        s = lax.dot_general(q, k, (((1,), (1,)), ((), ())), preferred_element_type=F32)
        m_prev = m_ref[...]
        m_new = jnp.maximum(m_prev, jnp.max(s, axis=-1, keepdims=True))
        alpha = jnp.exp2(m_prev - m_new)
        p = jnp.exp2(s - jnp.concatenate([m_new] * (s.shape[1] // LANES), axis=1)).astype(BF16)
        pv = jnp.dot(p, v, preferred_element_type=F32)
        acc_ref[...] = acc_ref[...] * jnp.concatenate([alpha, alpha], axis=1) + pv
        m_ref[...] = m_new

    for si, (bk, n) in enumerate(chunks):
        k_ref, v_ref = refs[2 * si], refs[2 * si + 1]
        if n == 1:
            step(k_ref[...], v_ref[...])
        else:
            def body(c, carry, k_ref=k_ref, v_ref=v_ref, bk=bk):
                off = pl.multiple_of(c * bk, bk)
                step(k_ref[pl.ds(off, bk), :], v_ref[pl.ds(off, bk), :])
                return carry

            lax.fori_loop(0, n, body, 0)
    acc = acc_ref[...]
    o_ref[...] = (acc[:, :V_HEAD] / acc[:, V_HEAD:]).astype(o_ref.dtype)


def _attention(q, sources, *, bq, bk):
    b, h, l, _ = q.shape
    bq = min(bq, l)
    chunks = []
    in_specs = [pl.BlockSpec((None, None, bq, HEAD_PAD), lambda bi, hi, i: (bi, hi, i, 0))]
    args = [q]
    for k, v in sources:
        lk = k.shape[2]
        cb = min(bk, lk)
        chunks.append((cb, lk // cb))
        for a in (k, v):
            in_specs.append(pl.BlockSpec((None, None, lk, HEAD_PAD), lambda bi, hi, i: (bi, hi, 0, 0)))
            args.append(a)
    return pl.pallas_call(
        functools.partial(_attn_kernel, chunks=tuple(chunks)),
        grid=(b, h, l // bq),
        in_specs=in_specs,
        out_specs=pl.BlockSpec((None, bq, V_HEAD), lambda bi, hi, i: (bi, i, hi)),
        out_shape=jax.ShapeDtypeStruct((b, l, h * V_HEAD), BF16),
        scratch_shapes=[pltpu.VMEM((bq, HEAD_PAD), F32), pltpu.VMEM((bq, LANES), F32)],
        compiler_params=_cparams("parallel", "parallel", "arbitrary"),
        name="latent_attention",
    )(*args)


def _merge_kernel(p_ref, a_ref, s_ref, gp_ref, ga_ref, gs_ref, wp_ref, wa_ref, ws_ref, o_ref):
    y = gp_ref[...].astype(F32) * jnp.dot(p_ref[...], wp_ref[...], preferred_element_type=F32)
    y += ga_ref[...].astype(F32) * jnp.dot(a_ref[...], wa_ref[...], preferred_element_type=F32)
    y += gs_ref[...].astype(F32) * jnp.dot(s_ref[...], ws_ref[...], preferred_element_type=F32)
    o_ref[...] = y.astype(o_ref.dtype)


def _merge(pool_out, attn, sg_out, gates, w_p, w_a, w_s, *, tm, tn):
    b, l, _ = pool_out.shape
    d = w_p.shape[1]
    tm = min(tm, l)
    nj = d // tn

    def act(width):
        return pl.BlockSpec((None, tm, width), lambda bi, i, j: (bi, i, 0))

    def gate(branch):
        return pl.BlockSpec((None, tm, tn), lambda bi, i, j: (bi, i, branch * nj + j))

    def wgt(width):
        return pl.BlockSpec((width, tn), lambda bi, i, j: (0, j))

    return pl.pallas_call(
        _merge_kernel,
        grid=(b, l // tm, nj),
        in_specs=[act(POOL_W), act(N_HEADS * V_HEAD), act(SG_W), gate(0), gate(1), gate(2),
                  wgt(POOL_W), wgt(N_HEADS * V_HEAD), wgt(SG_W)],
        out_specs=pl.BlockSpec((None, tm, tn), lambda bi, i, j: (bi, i, j)),
        out_shape=jax.ShapeDtypeStruct((b, l, d), BF16),
        compiler_params=_cparams("parallel", "parallel", "arbitrary"),
        name="merge_branches",
    )(pool_out, attn, sg_out, gates, gates, gates, w_p, w_a, w_s)


def _mm_res_kernel(a_ref, w_ref, x_ref, gate_ref, o_ref):
    y = jnp.dot(a_ref[...], w_ref[...], preferred_element_type=F32)
    o_ref[...] = x_ref[...] + gate_ref[...] * y


def _mm_residual(a, w, x, mod, gate_row, *, tm, tn, name):
    b, l, k = a.shape
    d = w.shape[1]
    tm = min(tm, l)
    return pl.pallas_call(
        _mm_res_kernel,
        grid=(b, l // tm, d // tn),
        in_specs=[
            pl.BlockSpec((None, tm, k), lambda bi, i, j: (bi, i, 0)),
            pl.BlockSpec((k, tn), lambda bi, i, j: (0, j)),
            pl.BlockSpec((None, tm, tn), lambda bi, i, j: (bi, i, j)),
            pl.BlockSpec((None, 1, tn), lambda bi, i, j: (bi, 0, j)),
        ],
        out_specs=pl.BlockSpec((None, tm, tn), lambda bi, i, j: (bi, i, j)),
        out_shape=jax.ShapeDtypeStruct((b, l, d), F32),
        compiler_params=_cparams("parallel", "parallel", "arbitrary"),
        name=name,
    )(a, w, x, mod[:, gate_row:gate_row + 1, :])


def _ffn_up_kernel(xp_ref, xc_ref, xn_ref, g_ref, mod_ref, wa_ref, wv_ref, cwa_ref, cwv_ref, cba_ref, cbv_ref,
                   o_ref, h_ref, ua_ref, uv_ref, *, tm, nt):
    ti = pl.program_id(1)
    halo = BF16_ROWS

    @pl.when(pl.program_id(2) == 0)
    def _():
        g = g_ref[...]
        h_ref[halo:halo + tm, :] = _norm_mod(xc_ref[...], g, mod_ref, 3, 4).astype(BF16)
        h_ref[0:halo, :] = jnp.where(ti > 0, _norm_mod(xp_ref[...], g, mod_ref, 3, 4), 0.0).astype(BF16)
        h_ref[halo + tm:, :] = jnp.where(ti < nt - 1, _norm_mod(xn_ref[...], g, mod_ref, 3, 4), 0.0).astype(BF16)

    h = h_ref[...]
    ua_ref[...] = jnp.dot(h, wa_ref[...], preferred_element_type=F32)
    uv_ref[...] = jnp.dot(h, wv_ref[...], preferred_element_type=F32)

    def conv(u_ref, cw_ref, cb_ref):
        acc = cb_ref[...] + u_ref[halo - 1:halo - 1 + tm, :] * cw_ref[0:1, :]
        acc = acc + u_ref[halo:halo + tm, :] * cw_ref[1:2, :]
        return acc + u_ref[halo + 1:halo + 1 + tm, :] * cw_ref[2:3, :]

    a = conv(ua_ref, cwa_ref, cba_ref)
    v = conv(uv_ref, cwv_ref, cbv_ref)
    o_ref[...] = (a * jax.nn.sigmoid(a) * v).astype(o_ref.dtype)


def _ffn_up(x, g, mod, w_up, conv_w, conv_b, *, tm, tf):
    b, l, d = x.shape
    d_ff = w_up.shape[1] // 2
    tm = min(tm, l)
    nt = l // tm
    nf = d_ff // tf
    hb = tm // BF16_ROWS
    return pl.pallas_call(
        functools.partial(_ffn_up_kernel, tm=tm, nt=nt),
        grid=(b, nt, nf),
        in_specs=[
            pl.BlockSpec((None, BF16_ROWS, d), lambda bi, i, j: (bi, jnp.maximum(i * hb - 1, 0), 0)),
            pl.BlockSpec((None, tm, d), lambda bi, i, j: (bi, i, 0)),
            pl.BlockSpec((None, BF16_ROWS, d), lambda bi, i, j: (bi, jnp.minimum((i + 1) * hb, l // BF16_ROWS - 1), 0)),
            pl.BlockSpec((1, d), lambda bi, i, j: (0, 0)),
            pl.BlockSpec((None, 6, d), lambda bi, i, j: (bi, 0, 0)),
            pl.BlockSpec((d, tf), lambda bi, i, j: (0, j)),
            pl.BlockSpec((d, tf), lambda bi, i, j: (0, nf + j)),
            pl.BlockSpec((CONV_W, tf), lambda bi, i, j: (0, j)),
            pl.BlockSpec((CONV_W, tf), lambda bi, i, j: (0, nf + j)),
            pl.BlockSpec((1, tf), lambda bi, i, j: (0, j)),
            pl.BlockSpec((1, tf), lambda bi, i, j: (0, nf + j)),
        ],
        out_specs=pl.BlockSpec((None, tm, tf), lambda bi, i, j: (bi, i, j)),
        out_shape=jax.ShapeDtypeStruct((b, l, d_ff), BF16),
        scratch_shapes=[
            pltpu.VMEM((tm + 2 * BF16_ROWS, d), BF16),
            pltpu.VMEM((tm + 2 * BF16_ROWS, tf), F32),
            pltpu.VMEM((tm + 2 * BF16_ROWS, tf), F32),
        ],
        compiler_params=_cparams("parallel", "parallel", "arbitrary"),
        name="ffn_up_conv_gate",
    )(x, x, x, g.reshape(1, d), mod, w_up, w_up, conv_w, conv_w, conv_b.reshape(1, -1), conv_b.reshape(1, -1))


def _rope_tables(n_tokens):
    t = jnp.arange(n_tokens)
    row = (t // GRID_W).astype(F32)
    col = (t % GRID_W).astype(F32)
    inv_freq = ROPE_THETA ** (-jnp.arange(ROPE_FREQS, dtype=F32) / ROPE_FREQS)
    ar = row[:, None] * inv_freq
    ac = col[:, None] * inv_freq
    c = jnp.concatenate([jnp.cos(ar), jnp.cos(ar), jnp.cos(ac), jnp.cos(ac)], axis=1)
    s = jnp.concatenate([-jnp.sin(ar), jnp.sin(ar), -jnp.sin(ac), jnp.sin(ac)], axis=1)
    return c, s


def _identity_tables(n_tokens):
    return jnp.ones((n_tokens, QK_ROPE), F32), jnp.zeros((n_tokens, QK_ROPE), F32)


def _layer_weights(p):
    d = p["w_in"].shape[0]
    w_in = p["w_in"]
    o_q, o_kv, o_kr, o_sg, o_gate = POOL_W, POOL_W + Q_LORA, POOL_W + Q_LORA + KV_LORA, \
        POOL_W + Q_LORA + KV_LORA + QK_ROPE, POOL_W + Q_LORA + KV_LORA + QK_ROPE + 2 * SG_W
    w_kr = w_in[:, o_kr:o_sg]
    w_small = jnp.concatenate([
        w_in[:, :o_q], w_in[:, o_q:o_kv], w_in[:, o_sg:o_gate], w_in[:, o_kv:o_kr], w_kr, w_kr[:, ROPE_SWAP],
        jnp.zeros((d, ZS_COLS - ZS_KV - KV_LORA - 2 * QK_ROPE), w_in.dtype)], axis=1).astype(BF16)
    w_gate = w_in[:, o_gate:].astype(BF16)

    w_uq = p["w_uq"].reshape(Q_LORA, N_HEADS, QK_HEAD)
    w_q_rope = w_uq[:, :, QK_NOPE:]
    w_q = jnp.concatenate([
        w_uq[:, :, :QK_NOPE].reshape(Q_LORA, -1), w_q_rope.reshape(Q_LORA, -1),
        w_q_rope[:, :, ROPE_SWAP].reshape(Q_LORA, -1)], axis=1).astype(BF16)
    gq = p["q_norm_g"]
    gq_n = gq[:QK_NOPE].reshape(1, QK_NOPE)
    gq_r = jnp.tile(gq[QK_NOPE:], N_HEADS).reshape(1, -1)
    gq_rs = jnp.tile(gq[QK_NOPE:][ROPE_SWAP], N_HEADS).reshape(1, -1)

    w_ukv = p["w_ukv"].reshape(KV_LORA, N_HEADS, QK_NOPE + V_HEAD)
    w_kv = jnp.concatenate([
        w_ukv[:, :, :QK_NOPE].reshape(KV_LORA, -1), w_ukv[:, :, QK_NOPE:].reshape(KV_LORA, -1)], axis=1).astype(BF16)
    gk = p["k_norm_g"]
    gk_n = gk[:QK_NOPE].reshape(1, QK_NOPE)
    gk_t = jnp.concatenate([gk[QK_NOPE:], gk[QK_NOPE:][ROPE_SWAP]]).reshape(1, LANES)
    return dict(
        w_small=w_small, w_gate=w_gate, w_q=w_q, gq_n=gq_n, gq_r=gq_r, gq_rs=gq_rs, w_kv=w_kv, gk_n=gk_n, gk_t=gk_t,
        pool_w=p["pool_w"].astype(BF16), sg_w=p["sg_w"].astype(BF16),
        w_br_pool=p["w_br_pool"].astype(BF16), w_br_mla=p["w_br_mla"].astype(BF16),
        w_br_sg=p["w_br_sg"].astype(BF16), w_o=p["w_o"].astype(BF16),
        ffn_up=p["ffn_up"].astype(BF16), ffn_down=p["ffn_down"].astype(BF16))


def _in_proj(x, p, w, mod):
    zs = _nm_matmul(x, p["norm1_g"], mod, w["w_small"], sigmoid=False, out_dtype=F32, tm=512, tn=512)
    gates = _nm_matmul(x, p["norm1_g"], mod, w["w_gate"], sigmoid=True, out_dtype=BF16, tm=512, tn=512)
    return zs, gates


def _keys_values(zs, p, w, tabs):
    c, s = tabs
    return _kv_proj(zs, p["kv_lat_g"], w["w_kv"], w["gk_n"], w["gk_t"], jnp.concatenate([c, s], axis=1), tm=512)


def _queries(zs, p, w, tabs):
    c, s = tabs
    out_scale = QK_HEAD ** -0.5 * math.log2(math.e)
    return _q_proj(zs, p["q_lat_g"], w["w_q"], w["gq_n"], w["gq_r"], w["gq_rs"], jnp.concatenate([c, c], axis=1),
                   jnp.concatenate([s, s], axis=1), out_scale=out_scale, tm=512)


def _mix_and_ffn(x, zs, gates, attn, p, w, mod):
    pool_out = _pool_mix(zs, w["pool_w"], p["pool_scale"], tm=512)
    sg_out = _spatial_gating(zs, p["sg_norm_g"], w["sg_w"], p["sg_b"], tm=512)
    y = _merge(pool_out, attn, sg_out, gates, w["w_br_pool"], w["w_br_mla"], w["w_br_sg"], tm=512, tn=512)
    x = _mm_residual(y, w["w_o"], x, mod, 2, tm=512, tn=512, name="out_proj_residual")
    act = _ffn_up(x, p["norm2_g"], mod, w["ffn_up"], p["ffn_conv_w"], p["ffn_conv_b"], tm=1024, tf=512)
    return _mm_residual(act, w["ffn_down"], x, mod, 5, tm=512, tn=512, name="ffn_down_residual")


def kernel(x, c, ctx, c_ctx, ada_w, ada_b, norm1_g, w_in, pool_w, pool_scale, q_lat_g, w_uq, kv_lat_g, w_ukv, q_norm_g, k_norm_g, sg_norm_g, sg_w, sg_b, w_br_pool, w_br_mla, w_br_sg, w_o, norm2_g, ffn_up, ffn_conv_w, ffn_conv_b, ffn_down):
    b, l, d = x.shape
    lc = ctx.shape[1]
    depth = ada_w.shape[0]
    assert b + 1 <= SUBLANES

    cvec = jnp.concatenate([c, c_ctx[None, :], jnp.zeros((SUBLANES - b - 1, d), c.dtype)], axis=0)
    mod = _ada(cvec, ada_w, ada_b).reshape(depth, SUBLANES, 6, d)
    rope_lat = _rope_tables(l)
    rope_ctx = _identity_tables(lc)

    xl, xc = x, ctx
    for i in range(depth):
        p = {
            "norm1_g": norm1_g[i], "w_in": w_in[i], "pool_w": pool_w[i], "pool_scale": pool_scale[i],
            "q_lat_g": q_lat_g[i], "w_uq": w_uq[i], "kv_lat_g": kv_lat_g[i], "w_ukv": w_ukv[i],
            "q_norm_g": q_norm_g[i], "k_norm_g": k_norm_g[i], "sg_norm_g": sg_norm_g[i], "sg_w": sg_w[i],
            "sg_b": sg_b[i], "w_br_pool": w_br_pool[i], "w_br_mla": w_br_mla[i], "w_br_sg": w_br_sg[i],
            "w_o": w_o[i], "norm2_g": norm2_g[i], "ffn_up": ffn_up[i], "ffn_conv_w": ffn_conv_w[i],
            "ffn_conv_b": ffn_conv_b[i], "ffn_down": ffn_down[i],
        }
        w = _layer_weights(p)
        mod_lat = mod[i, :b]
        mod_ctx = jnp.broadcast_to(mod[i, b:b + 1], (b, 6, d))

        zl, gl = _in_proj(xl, p, w, mod_lat)
        zc, gc = _in_proj(xc, p, w, mod_ctx)
        kc, vc = _keys_values(zc, p, w, rope_ctx)
        kl, vl = _keys_values(zl, p, w, rope_lat)
        ql = _queries(zl, p, w, rope_lat)
        attn_l = _attention(ql, [(kl, vl), (kc, vc)], bq=512, bk=512)
        xl = _mix_and_ffn(xl, zl, gl, attn_l, p, w, mod_lat)
        if i < depth - 1:
            qc = _queries(zc, p, w, rope_ctx)
            attn_c = _attention(qc, [(kc, vc)], bq=512, bk=512)
            xc = _mix_and_ffn(xc, zc, gc, attn_c, p, w, mod_ctx)
    return xl
```

```python
import functools
import math

import jax
import jax.numpy as jnp
import numpy as np
from jax import lax
from jax.experimental import pallas as pl
from jax.experimental.pallas import tpu as pltpu

F32 = jnp.float32
BF16 = jnp.bfloat16

GRID_W = 64
EPS = 1e-6
POOL_W = 512
POOL_WINDOWS = (2, 4, 8, 16)
POOL_GROUP_W = POOL_W // len(POOL_WINDOWS)
N_HEADS = 8
Q_LORA = 512
KV_LORA = 256
QK_NOPE = 128
QK_ROPE = 64
QK_HEAD = QK_NOPE + QK_ROPE
V_HEAD = 128
ROPE_FREQS = QK_ROPE // 4
ROPE_THETA = 10000.0
SG_W = 512
SG_GROUPS = 4
SG_GROUP_W = SG_W // SG_GROUPS
CHUNK = 128
N_BRANCH = 3
CONV_W = 3

LANES = 128
SUBLANES = 8
BF16_ROWS = 16
HEAD_PAD = 2 * LANES
VMEM_LIMIT_BYTES = 56 * 1024 * 1024

ZS_POOL = 0
ZS_Q = 512
ZS_SG = 1024
ZS_KV = 2048
ZS_COLS = 2560
ROPE_SWAP = np.concatenate([np.arange(16, 32), np.arange(0, 16), np.arange(48, 64), np.arange(32, 48)])


def _cparams(*sem):
    return pltpu.CompilerParams(dimension_semantics=sem, vmem_limit_bytes=VMEM_LIMIT_BYTES)


def _rms_scale(x, width):
    return lax.rsqrt(jnp.sum(x * x, axis=-1, keepdims=True) * (1.0 / width) + EPS)


def _ada_kernel(c_ref, w_ref, b_ref, o_ref):
    c = c_ref[...]
    s = c * jax.nn.sigmoid(c)
    o_ref[...] = jnp.dot(s, w_ref[...], preferred_element_type=F32, precision=lax.Precision.HIGHEST) + b_ref[...]


def _ada(cvec, ada_w, ada_b):
    depth, d, n = ada_w.shape
    tn = 1024
    return pl.pallas_call(
        _ada_kernel,
        grid=(depth, n // tn),
        in_specs=[
            pl.BlockSpec((SUBLANES, d), lambda l, j: (0, 0)),
            pl.BlockSpec((None, d, tn), lambda l, j: (l, 0, j)),
            pl.BlockSpec((None, 1, tn), lambda l, j: (l, 0, j)),
        ],
        out_specs=pl.BlockSpec((None, SUBLANES, tn), lambda l, j: (l, 0, j)),
        out_shape=jax.ShapeDtypeStruct((depth, SUBLANES, n), F32),
        compiler_params=_cparams("parallel", "arbitrary"),
        name="ada_modulation",
    )(cvec, ada_w, ada_b.reshape(depth, 1, n))


def _norm_mod(x, g, mod_ref, shift_row, scale_row):
    y = x * _rms_scale(x, x.shape[-1]) * g
    return y * (1.0 + mod_ref[scale_row:scale_row + 1, :]) + mod_ref[shift_row:shift_row + 1, :]


def _sigmoid(x):
    return 0.5 * jnp.tanh(0.5 * x) + 0.5


def _in_proj_kernel(x_ref, g_ref, mod_ref, w_ref, zs_ref, gate_ref, h_ref, *, n_small):
    j = pl.program_id(2)

    @pl.when(j == 0)
    def _():
        h_ref[...] = _norm_mod(x_ref[...], g_ref[...], mod_ref, 0, 1).astype(BF16)

    z = jnp.dot(h_ref[...], w_ref[...], preferred_element_type=F32)

    @pl.when(j < n_small)
    def _():
        zs_ref[...] = z

    @pl.when(j >= n_small)
    def _():
        gate_ref[...] = _sigmoid(z).astype(gate_ref.dtype)


def _in_proj(x, g, mod, w, *, tm, tn):
    b, l, d = x.shape
    n = w.shape[1]
    tm = min(tm, l)
    n_small = ZS_COLS // tn
    return pl.pallas_call(
        functools.partial(_in_proj_kernel, n_small=n_small),
        grid=(b, l // tm, n // tn),
        in_specs=[
            pl.BlockSpec((None, tm, d), lambda bi, i, j: (bi, i, 0)),
            pl.BlockSpec((1, d), lambda bi, i, j: (0, 0)),
            pl.BlockSpec((None, 6, d), lambda bi, i, j: (bi, 0, 0)),
            pl.BlockSpec((d, tn), lambda bi, i, j: (0, j)),
        ],
        out_specs=[
            pl.BlockSpec((None, tm, tn), lambda bi, i, j: (bi, i, jnp.minimum(j, n_small - 1))),
            pl.BlockSpec((None, tm, tn), lambda bi, i, j: (bi, i, jnp.maximum(j - n_small, 0))),
        ],
        out_shape=[jax.ShapeDtypeStruct((b, l, ZS_COLS), F32), jax.ShapeDtypeStruct((b, l, n - ZS_COLS), BF16)],
        scratch_shapes=[pltpu.VMEM((tm, d), BF16)],
        compiler_params=_cparams("parallel", "parallel", "arbitrary"),
        name="in_proj",
    )(x, g.reshape(1, d), mod, w)


def _pool_kernel(prev_ref, cur_ref, next_ref, pw_ref, ps_ref, o_ref, ext_ref, *, tm, nt, seq_len):
    ti = pl.program_id(1)
    cur = cur_ref[...]
    ext_ref[SUBLANES:SUBLANES + tm, :] = cur
    ext_ref[0:SUBLANES, :] = jnp.where(ti > 0, prev_ref[...], 0.0)
    ext_ref[SUBLANES + tm:, :] = jnp.where(ti < nt - 1, next_ref[...], 0.0)
    t = ti * tm + lax.broadcasted_iota(jnp.int32, (tm, 1), 0)
    for g, w in enumerate(POOL_WINDOWS):
        r = w // 2
        c0 = g * POOL_GROUP_W
        s = ext_ref[SUBLANES - r:SUBLANES - r + tm, c0:c0 + POOL_GROUP_W]
        for dlt in range(-r + 1, r):
            s = s + ext_ref[SUBLANES + dlt:SUBLANES + dlt + tm, c0:c0 + POOL_GROUP_W]
        cnt = (jnp.minimum(t + r, seq_len) - jnp.maximum(t - r, 0)).astype(F32)
        pooled = s / cnt - cur[:, c0:c0 + POOL_GROUP_W]
        y = jnp.dot(pooled.astype(BF16), pw_ref[g], preferred_element_type=F32)
        o_ref[:, c0:c0 + POOL_GROUP_W] = (y * ps_ref[:, c0:c0 + POOL_GROUP_W]).astype(o_ref.dtype)


def _pool_mix(zs, pool_w, pool_scale, *, tm):
    b, l, _ = zs.shape
    tm = min(tm, l)
    nt = l // tm
    hb = tm // SUBLANES
    return pl.pallas_call(
        functools.partial(_pool_kernel, tm=tm, nt=nt, seq_len=l),
        grid=(b, nt),
        in_specs=[
            pl.BlockSpec((None, SUBLANES, POOL_W), lambda bi, i: (bi, jnp.maximum(i * hb - 1, 0), 0)),
            pl.BlockSpec((None, tm, POOL_W), lambda bi, i: (bi, i, 0)),
            pl.BlockSpec((None, SUBLANES, POOL_W), lambda bi, i: (bi, jnp.minimum((i + 1) * hb, l // SUBLANES - 1), 0)),
            pl.BlockSpec((len(POOL_WINDOWS), POOL_GROUP_W, POOL_GROUP_W), lambda bi, i: (0, 0, 0)),
            pl.BlockSpec((1, POOL_W), lambda bi, i: (0, 0)),
        ],
        out_specs=pl.BlockSpec((None, tm, POOL_W), lambda bi, i: (bi, i, 0)),
        out_shape=jax.ShapeDtypeStruct((b, l, POOL_W), BF16),
        scratch_shapes=[pltpu.VMEM((tm + 2 * SUBLANES, POOL_W), F32)],
        compiler_params=_cparams("parallel", "parallel"),
        name="pool_mix",
    )(zs, zs, zs, pool_w, pool_scale.reshape(1, POOL_W))


def _sg_kernel(z_ref, g_ref, w_ref, b_ref, o_ref, *, tm):
    z = z_ref[...]
    a = z * (0.5 * (1.0 + jnp.tanh(math.sqrt(2.0 / math.pi) * (z + 0.044715 * (z * z * z)))))
    u = a[:, :SG_W]
    v = a[:, SG_W:]
    vc = v - jnp.mean(v, axis=-1, keepdims=True)
    vn = (vc * lax.rsqrt(jnp.mean(vc * vc, axis=-1, keepdims=True) + EPS) * g_ref[...]).astype(BF16)
    for c in range(tm // CHUNK):
        rows = slice(c * CHUNK, (c + 1) * CHUNK)
        for g in range(SG_GROUPS):
            cols = slice(g * SG_GROUP_W, (g + 1) * SG_GROUP_W)
            s = jnp.dot(w_ref[g], vn[rows, cols], preferred_element_type=F32) + b_ref[g]
            o_ref[rows, cols] = (u[rows, cols] * s).astype(o_ref.dtype)


def _spatial_gating(zs, sg_norm_g, sg_w, sg_b, *, tm):
    b, l, _ = zs.shape
    tm = min(tm, l)
    return pl.pallas_call(
        functools.partial(_sg_kernel, tm=tm),
        grid=(b, l // tm),
        in_specs=[
            pl.BlockSpec((None, tm, 2 * SG_W), lambda bi, i: (bi, i, ZS_SG // (2 * SG_W))),
            pl.BlockSpec((1, SG_W), lambda bi, i: (0, 0)),
            pl.BlockSpec((SG_GROUPS, CHUNK, CHUNK), lambda bi, i: (0, 0, 0)),
            pl.BlockSpec((SG_GROUPS, CHUNK, 1), lambda bi, i: (0, 0, 0)),
        ],
        out_specs=pl.BlockSpec((None, tm, SG_W), lambda bi, i: (bi, i, 0)),
        out_shape=jax.ShapeDtypeStruct((b, l, SG_W), BF16),
        compiler_params=_cparams("parallel", "parallel"),
        name="spatial_gating",
    )(zs, sg_norm_g.reshape(1, SG_W), sg_w, sg_b.reshape(SG_GROUPS, CHUNK, 1))


def _q_kernel(z_ref, gl_ref, w_ref, gn_ref, gr_ref, grs_ref, cos_ref, sin_ref, o_ref, *, out_scale):
    z = z_ref[...]
    zn = (z * _rms_scale(z, Q_LORA) * gl_ref[...]).astype(BF16)
    q = jnp.dot(zn, w_ref[...], preferred_element_type=F32)
    nope_w = N_HEADS * QK_NOPE
    rope_w = N_HEADS * QK_ROPE
    lo = lax.broadcasted_iota(jnp.int32, (1, LANES), 1) < QK_ROPE
    cos = cos_ref[...]
    sin = sin_ref[...]
    gn = gn_ref[...] * out_scale
    for pair in range(N_HEADS // 2):
        blk = slice(pair * LANES, (pair + 1) * LANES)
        r2 = q[:, nope_w + pair * LANES:nope_w + (pair + 1) * LANES]
        r2s = q[:, nope_w + rope_w + pair * LANES:nope_w + rope_w + (pair + 1) * LANES]
        sq = r2 * r2
        ss_lo = jnp.sum(jnp.where(lo, sq, 0.0), axis=-1, keepdims=True)
        ss_hi = jnp.sum(jnp.where(lo, 0.0, sq), axis=-1, keepdims=True)
        rot = r2 * gr_ref[:, blk] * cos + r2s * grs_ref[:, blk] * sin
        for half, ss_r in enumerate((ss_lo, ss_hi)):
            h = 2 * pair + half
            nope = q[:, h * QK_NOPE:(h + 1) * QK_NOPE]
            inv = lax.rsqrt((jnp.sum(nope * nope, axis=-1, keepdims=True) + ss_r) * (1.0 / QK_HEAD) + EPS)
            o_ref[h, :, 0:LANES] = (nope * inv * gn).astype(o_ref.dtype)
            keep = lo if half == 0 else jnp.logical_not(lo)
            o_ref[h, :, LANES:HEAD_PAD] = jnp.where(keep, rot * (inv * out_scale), 0.0).astype(o_ref.dtype)


def _q_proj(zs, q_lat_g, w_q, gn, gr, grs, cos_tab, sin_tab, *, out_scale, tm):
    b, l, _ = zs.shape
    tm = min(tm, l)
    return pl.pallas_call(
        functools.partial(_q_kernel, out_scale=out_scale),
        grid=(b, l // tm),
        in_specs=[
            pl.BlockSpec((None, tm, Q_LORA), lambda bi, i: (bi, i, ZS_Q // Q_LORA)),
            pl.BlockSpec((1, Q_LORA), lambda bi, i: (0, 0)),
            pl.BlockSpec(w_q.shape, lambda bi, i: (0, 0)),
            pl.BlockSpec((1, QK_NOPE), lambda bi, i: (0, 0)),
            pl.BlockSpec((1, N_HEADS * QK_ROPE), lambda bi, i: (0, 0)),
            pl.BlockSpec((1, N_HEADS * QK_ROPE), lambda bi, i: (0, 0)),
            pl.BlockSpec((tm, LANES), lambda bi, i: (i, 0)),
            pl.BlockSpec((tm, LANES), lambda bi, i: (i, 0)),
        ],
        out_specs=pl.BlockSpec((None, N_HEADS, tm, HEAD_PAD), lambda bi, i: (bi, 0, i, 0)),
        out_shape=jax.ShapeDtypeStruct((b, N_HEADS, l, HEAD_PAD), BF16),
        compiler_params=_cparams("parallel", "parallel"),
        name="q_proj",
    )(zs, q_lat_g.reshape(1, Q_LORA), w_q, gn, gr, grs, cos_tab, sin_tab)


def _kv_kernel(z_ref, gl_ref, w_ref, gn_ref, gt_ref, tab_ref, k_ref, v_ref):
    z = z_ref[...]
    zkv = z[:, :KV_LORA]
    zn = (zkv * _rms_scale(zkv, KV_LORA) * gl_ref[...]).astype(BF16)
    kv = jnp.dot(zn, w_ref[...], preferred_element_type=F32)
    lo = lax.broadcasted_iota(jnp.int32, (1, LANES), 1) < QK_ROPE
    kr2 = z[:, KV_LORA:KV_LORA + LANES]
    ss_r = jnp.sum(jnp.where(lo, kr2 * kr2, 0.0), axis=-1, keepdims=True)
    t = kr2 * gt_ref[...] * tab_ref[...]
    rot = t + pltpu.roll(t, QK_ROPE, axis=1)
    gn = gn_ref[...]
    ones = jnp.ones((z.shape[0], LANES), v_ref.dtype)
    for h in range(N_HEADS):
        nope = kv[:, h * QK_NOPE:(h + 1) * QK_NOPE]
        inv = lax.rsqrt((jnp.sum(nope * nope, axis=-1, keepdims=True) + ss_r) * (1.0 / QK_HEAD) + EPS)
        k_ref[h, :, 0:LANES] = (nope * inv * gn).astype(k_ref.dtype)
        keep = lo if h % 2 == 0 else jnp.logical_not(lo)
        k_ref[h, :, LANES:HEAD_PAD] = jnp.where(keep, rot * inv, 0.0).astype(k_ref.dtype)
        v_ref[h, :, 0:LANES] = kv[:, N_HEADS * QK_NOPE + h * V_HEAD:N_HEADS * QK_NOPE + (h + 1) * V_HEAD].astype(
            v_ref.dtype)
        v_ref[h, :, LANES:HEAD_PAD] = ones


def _kv_proj(zs, kv_lat_g, w_kv, gn, gt, tab, *, tm):
    b, l, _ = zs.shape
    tm = min(tm, l)
    blk = KV_LORA + 2 * LANES
    shape = jax.ShapeDtypeStruct((b, N_HEADS, l, HEAD_PAD), BF16)
    spec = pl.BlockSpec((None, N_HEADS, tm, HEAD_PAD), lambda bi, i: (bi, 0, i, 0))
    return pl.pallas_call(
        _kv_kernel,
        grid=(b, l // tm),
        in_specs=[
            pl.BlockSpec((None, tm, blk), lambda bi, i: (bi, i, ZS_KV // blk)),
            pl.BlockSpec((1, KV_LORA), lambda bi, i: (0, 0)),
            pl.BlockSpec(w_kv.shape, lambda bi, i: (0, 0)),
            pl.BlockSpec((1, QK_NOPE), lambda bi, i: (0, 0)),
            pl.BlockSpec((1, LANES), lambda bi, i: (0, 0)),
            pl.BlockSpec((tm, LANES), lambda bi, i: (i, 0)),
        ],
        out_specs=[spec, spec],
        out_shape=[shape, shape],
        compiler_params=_cparams("parallel", "parallel"),
        name="kv_proj",
    )(zs, kv_lat_g.reshape(1, KV_LORA), w_kv, gn, gt, tab)


def _attn_kernel(q_ref, *refs, chunks):
    n_src = len(chunks)
    o_ref, acc_ref, m_ref = refs[2 * n_src:]
    q = q_ref[...]
    acc_ref[...] = jnp.zeros_like(acc_ref)
    m_ref[...] = jnp.full_like(m_ref, -jnp.inf)

    def step(k, v):
        s = lax.dot_general(q, k, (((1,), (1,)), ((), ())), preferred_element_type=F32)
        m_prev = m_ref[...]
        m_new = jnp.maximum(m_prev, jnp.max(s, axis=-1, keepdims=True))
        alpha = jnp.exp2(m_prev - m_new)
        p = jnp.exp2(s - jnp.concatenate([m_new] * (s.shape[1] // LANES), axis=1)).astype(BF16)
        pv = jnp.dot(p, v, preferred_element_type=F32)
        acc_ref[...] = acc_ref[...] * jnp.concatenate([alpha, alpha], axis=1) + pv
        m_ref[...] = m_new

    for si, (bk, n) in enumerate(chunks):
        k_ref, v_ref = refs[2 * si], refs[2 * si + 1]
        if n == 1:
            step(k_ref[...], v_ref[...])
        else:
            def body(c, carry, k_ref=k_ref, v_ref=v_ref, bk=bk):
                off = pl.multiple_of(c * bk, bk)
                step(k_ref[pl.ds(off, bk), :], v_ref[pl.ds(off, bk), :])
                return carry

            lax.fori_loop(0, n, body, 0, unroll=16)
    acc = acc_ref[...]
    o_ref[...] = (acc[:, :V_HEAD] / acc[:, V_HEAD:]).astype(o_ref.dtype)


def _attention(q, sources, *, bq, bk):
    b, h, l, _ = q.shape
    bq = min(bq, l)
    chunks = []
    in_specs = [pl.BlockSpec((None, None, bq, HEAD_PAD), lambda bi, hi, i: (bi, hi, i, 0))]
    args = [q]
    for k, v in sources:
        lk = k.shape[2]
        cb = min(bk, lk)
        chunks.append((cb, lk // cb))
        for a in (k, v):
            in_specs.append(pl.BlockSpec((None, None, lk, HEAD_PAD), lambda bi, hi, i: (bi, hi, 0, 0)))
            args.append(a)
    return pl.pallas_call(
        functools.partial(_attn_kernel, chunks=tuple(chunks)),
        grid=(b, h, l // bq),
        in_specs=in_specs,
        out_specs=pl.BlockSpec((None, bq, V_HEAD), lambda bi, hi, i: (bi, i, hi)),
        out_shape=jax.ShapeDtypeStruct((b, l, h * V_HEAD), BF16),
        scratch_shapes=[pltpu.VMEM((bq, HEAD_PAD), F32), pltpu.VMEM((bq, LANES), F32)],
        compiler_params=_cparams("parallel", "parallel", "arbitrary"),
        name="latent_attention",
    )(*args)


def _merge_kernel(p_ref, a_ref, s_ref, gp_ref, ga_ref, gs_ref, wp_ref, wa_ref, ws_ref, o_ref):
    y = gp_ref[...].astype(F32) * jnp.dot(p_ref[...], wp_ref[...], preferred_element_type=F32)
    y += ga_ref[...].astype(F32) * jnp.dot(a_ref[...], wa_ref[...], preferred_element_type=F32)
    y += gs_ref[...].astype(F32) * jnp.dot(s_ref[...], ws_ref[...], preferred_element_type=F32)
    o_ref[...] = y.astype(o_ref.dtype)


def _merge(pool_out, attn, sg_out, gates, w_p, w_a, w_s, *, tm, tn):
    b, l, _ = pool_out.shape
    d = w_p.shape[1]
    tm = min(tm, l)
    nj = d // tn

    def act(width):
        return pl.BlockSpec((None, tm, width), lambda bi, i, j: (bi, i, 0))

    def gate(branch):
        return pl.BlockSpec((None, tm, tn), lambda bi, i, j: (bi, i, branch * nj + j))

    def wgt(width):
        return pl.BlockSpec((width, tn), lambda bi, i, j: (0, j))

    return pl.pallas_call(
        _merge_kernel,
        grid=(b, l // tm, nj),
        in_specs=[act(POOL_W), act(N_HEADS * V_HEAD), act(SG_W), gate(0), gate(1), gate(2),
                  wgt(POOL_W), wgt(N_HEADS * V_HEAD), wgt(SG_W)],
        out_specs=pl.BlockSpec((None, tm, tn), lambda bi, i, j: (bi, i, j)),
        out_shape=jax.ShapeDtypeStruct((b, l, d), BF16),
        compiler_params=_cparams("parallel", "parallel", "arbitrary"),
        name="merge_branches",
    )(pool_out, attn, sg_out, gates, gates, gates, w_p, w_a, w_s)


def _mm_res_kernel(a_ref, w_ref, x_ref, gate_ref, o_ref):
    y = jnp.dot(a_ref[...], w_ref[...], preferred_element_type=F32)
    o_ref[...] = x_ref[...] + gate_ref[...] * y


def _mm_residual(a, w, x, mod, gate_row, *, tm, tn, name):
    b, l, k = a.shape
    d = w.shape[1]
    tm = min(tm, l)
    return pl.pallas_call(
        _mm_res_kernel,
        grid=(b, l // tm, d // tn),
        in_specs=[
            pl.BlockSpec((None, tm, k), lambda bi, i, j: (bi, i, 0)),
            pl.BlockSpec((k, tn), lambda bi, i, j: (0, j)),
            pl.BlockSpec((None, tm, tn), lambda bi, i, j: (bi, i, j)),
            pl.BlockSpec((None, 1, tn), lambda bi, i, j: (bi, 0, j)),
        ],
        out_specs=pl.BlockSpec((None, tm, tn), lambda bi, i, j: (bi, i, j)),
        out_shape=jax.ShapeDtypeStruct((b, l, d), F32),
        compiler_params=_cparams("parallel", "parallel", "arbitrary"),
        name=name,
    )(a, w, x, mod[:, gate_row:gate_row + 1, :])


def _ffn_up_kernel(xp_ref, xc_ref, xn_ref, g_ref, mod_ref, wa_ref, wv_ref, cwa_ref, cwv_ref, cba_ref, cbv_ref,
                   o_ref, h_ref, ua_ref, uv_ref, *, tm, nt):
    ti = pl.program_id(1)
    halo = BF16_ROWS

    @pl.when(pl.program_id(2) == 0)
    def _():
        g = g_ref[...]
        h_ref[halo:halo + tm, :] = _norm_mod(xc_ref[...], g, mod_ref, 3, 4).astype(BF16)
        h_ref[0:halo, :] = jnp.where(ti > 0, _norm_mod(xp_ref[...], g, mod_ref, 3, 4), 0.0).astype(BF16)
        h_ref[halo + tm:, :] = jnp.where(ti < nt - 1, _norm_mod(xn_ref[...], g, mod_ref, 3, 4), 0.0).astype(BF16)

    h = h_ref[...]
    ua_ref[...] = jnp.dot(h, wa_ref[...], preferred_element_type=F32)
    uv_ref[...] = jnp.dot(h, wv_ref[...], preferred_element_type=F32)

    def conv(u_ref, cw_ref, cb_ref):
        acc = cb_ref[...] + u_ref[halo - 1:halo - 1 + tm, :] * cw_ref[0:1, :]
        acc = acc + u_ref[halo:halo + tm, :] * cw_ref[1:2, :]
        return acc + u_ref[halo + 1:halo + 1 + tm, :] * cw_ref[2:3, :]

    a = conv(ua_ref, cwa_ref, cba_ref)
    v = conv(uv_ref, cwv_ref, cbv_ref)
    o_ref[...] = (a * _sigmoid(a) * v).astype(o_ref.dtype)


def _ffn_up(x, g, mod, w_up, conv_w, conv_b, *, tm, tf):
    b, l, d = x.shape
    d_ff = w_up.shape[1] // 2
    tm = min(tm, l)
    nt = l // tm
    nf = d_ff // tf
    hb = tm // BF16_ROWS
    return pl.pallas_call(
        functools.partial(_ffn_up_kernel, tm=tm, nt=nt),
        grid=(b, nt, nf),
        in_specs=[
            pl.BlockSpec((None, BF16_ROWS, d), lambda bi, i, j: (bi, jnp.maximum(i * hb - 1, 0), 0)),
            pl.BlockSpec((None, tm, d), lambda bi, i, j: (bi, i, 0)),
            pl.BlockSpec((None, BF16_ROWS, d), lambda bi, i, j: (bi, jnp.minimum((i + 1) * hb, l // BF16_ROWS - 1), 0)),
            pl.BlockSpec((1, d), lambda bi, i, j: (0, 0)),
            pl.BlockSpec((None, 6, d), lambda bi, i, j: (bi, 0, 0)),
            pl.BlockSpec((d, tf), lambda bi, i, j: (0, j)),
            pl.BlockSpec((d, tf), lambda bi, i, j: (0, nf + j)),
            pl.BlockSpec((CONV_W, tf), lambda bi, i, j: (0, j)),
            pl.BlockSpec((CONV_W, tf), lambda bi, i, j: (0, nf + j)),
            pl.BlockSpec((1, tf), lambda bi, i, j: (0, j)),
            pl.BlockSpec((1, tf), lambda bi, i, j: (0, nf + j)),
        ],
        out_specs=pl.BlockSpec((None, tm, tf), lambda bi, i, j: (bi, i, j)),
        out_shape=jax.ShapeDtypeStruct((b, l, d_ff), BF16),
        scratch_shapes=[
            pltpu.VMEM((tm + 2 * BF16_ROWS, d), BF16),
            pltpu.VMEM((tm + 2 * BF16_ROWS, tf), F32),
            pltpu.VMEM((tm + 2 * BF16_ROWS, tf), F32),
        ],
        compiler_params=_cparams("parallel", "parallel", "arbitrary"),
        name="ffn_up_conv_gate",
    )(x, x, x, g.reshape(1, d), mod, w_up, w_up, conv_w, conv_w, conv_b.reshape(1, -1), conv_b.reshape(1, -1))


def _rope_tables(n_tokens):
    t = jnp.arange(n_tokens)
    row = (t // GRID_W).astype(F32)
    col = (t % GRID_W).astype(F32)
    inv_freq = ROPE_THETA ** (-jnp.arange(ROPE_FREQS, dtype=F32) / ROPE_FREQS)
    ar = row[:, None] * inv_freq
    ac = col[:, None] * inv_freq
    c = jnp.concatenate([jnp.cos(ar), jnp.cos(ar), jnp.cos(ac), jnp.cos(ac)], axis=1)
    s = jnp.concatenate([-jnp.sin(ar), jnp.sin(ar), -jnp.sin(ac), jnp.sin(ac)], axis=1)
    return c, s


def _identity_tables(n_tokens):
    return jnp.ones((n_tokens, QK_ROPE), F32), jnp.zeros((n_tokens, QK_ROPE), F32)


def _layer_weights(p):
    d = p["w_in"].shape[0]
    w_in = p["w_in"]
    o_q, o_kv, o_kr, o_sg, o_gate = POOL_W, POOL_W + Q_LORA, POOL_W + Q_LORA + KV_LORA, \
        POOL_W + Q_LORA + KV_LORA + QK_ROPE, POOL_W + Q_LORA + KV_LORA + QK_ROPE + 2 * SG_W
    w_kr = w_in[:, o_kr:o_sg]
    w_in_r = jnp.concatenate([
        w_in[:, :o_q], w_in[:, o_q:o_kv], w_in[:, o_sg:o_gate], w_in[:, o_kv:o_kr], w_kr, w_kr[:, ROPE_SWAP],
        jnp.zeros((d, ZS_COLS - ZS_KV - KV_LORA - 2 * QK_ROPE), w_in.dtype), w_in[:, o_gate:]], axis=1).astype(BF16)

    w_uq = p["w_uq"].reshape(Q_LORA, N_HEADS, QK_HEAD)
    w_q_rope = w_uq[:, :, QK_NOPE:]
    w_q = jnp.concatenate([
        w_uq[:, :, :QK_NOPE].reshape(Q_LORA, -1), w_q_rope.reshape(Q_LORA, -1),
        w_q_rope[:, :, ROPE_SWAP].reshape(Q_LORA, -1)], axis=1).astype(BF16)
    gq = p["q_norm_g"]
    gq_n = gq[:QK_NOPE].reshape(1, QK_NOPE)
    gq_r = jnp.tile(gq[QK_NOPE:], N_HEADS).reshape(1, -1)
    gq_rs = jnp.tile(gq[QK_NOPE:][ROPE_SWAP], N_HEADS).reshape(1, -1)

    w_ukv = p["w_ukv"].reshape(KV_LORA, N_HEADS, QK_NOPE + V_HEAD)
    w_kv = jnp.concatenate([
        w_ukv[:, :, :QK_NOPE].reshape(KV_LORA, -1), w_ukv[:, :, QK_NOPE:].reshape(KV_LORA, -1)], axis=1).astype(BF16)
    gk = p["k_norm_g"]
    gk_n = gk[:QK_NOPE].reshape(1, QK_NOPE)
    gk_t = jnp.concatenate([gk[QK_NOPE:], gk[QK_NOPE:][ROPE_SWAP]]).reshape(1, LANES)
    return dict(
        w_in=w_in_r, w_q=w_q, gq_n=gq_n, gq_r=gq_r, gq_rs=gq_rs, w_kv=w_kv, gk_n=gk_n, gk_t=gk_t,
        pool_w=p["pool_w"].astype(BF16), sg_w=p["sg_w"].astype(BF16),
        w_br_pool=p["w_br_pool"].astype(BF16), w_br_mla=p["w_br_mla"].astype(BF16),
        w_br_sg=p["w_br_sg"].astype(BF16), w_o=p["w_o"].astype(BF16),
        ffn_up=p["ffn_up"].astype(BF16), ffn_down=p["ffn_down"].astype(BF16))


def _keys_values(zs, p, w, tabs):
    c, s = tabs
    return _kv_proj(zs, p["kv_lat_g"], w["w_kv"], w["gk_n"], w["gk_t"], jnp.concatenate([c, s], axis=1), tm=512)


def _queries(zs, p, w, tabs):
    c, s = tabs
    out_scale = QK_HEAD ** -0.5 * math.log2(math.e)
    return _q_proj(zs, p["q_lat_g"], w["w_q"], w["gq_n"], w["gq_r"], w["gq_rs"], jnp.concatenate([c, c], axis=1),
                   jnp.concatenate([s, s], axis=1), out_scale=out_scale, tm=512)


def _mix_and_ffn(x, zs, gates, attn, p, w, mod):
    pool_out = _pool_mix(zs, w["pool_w"], p["pool_scale"], tm=512)
    sg_out = _spatial_gating(zs, p["sg_norm_g"], w["sg_w"], p["sg_b"], tm=512)
    y = _merge(pool_out, attn, sg_out, gates, w["w_br_pool"], w["w_br_mla"], w["w_br_sg"], tm=1024, tn=512)
    x = _mm_residual(y, w["w_o"], x, mod, 2, tm=1024, tn=512, name="out_proj_residual")
    act = _ffn_up(x, p["norm2_g"], mod, w["ffn_up"], p["ffn_conv_w"], p["ffn_conv_b"], tm=1024, tf=512)
    return _mm_residual(act, w["ffn_down"], x, mod, 5, tm=1024, tn=512, name="ffn_down_residual")


def kernel(x, c, ctx, c_ctx, ada_w, ada_b, norm1_g, w_in, pool_w, pool_scale, q_lat_g, w_uq, kv_lat_g, w_ukv, q_norm_g, k_norm_g, sg_norm_g, sg_w, sg_b, w_br_pool, w_br_mla, w_br_sg, w_o, norm2_g, ffn_up, ffn_conv_w, ffn_conv_b, ffn_down):
    b, l, d = x.shape
    lc = ctx.shape[1]
    depth = ada_w.shape[0]
    assert b + 1 <= SUBLANES

    cvec = jnp.concatenate([c, c_ctx[None, :], jnp.zeros((SUBLANES - b - 1, d), c.dtype)], axis=0)
    mod = _ada(cvec, ada_w, ada_b).reshape(depth, SUBLANES, 6, d)
    rope_lat = _rope_tables(l)
    rope_ctx = _identity_tables(lc)

    xl, xc = x, ctx
    for i in range(depth):
        p = {
            "norm1_g": norm1_g[i], "w_in": w_in[i], "pool_w": pool_w[i], "pool_scale": pool_scale[i],
            "q_lat_g": q_lat_g[i], "w_uq": w_uq[i], "kv_lat_g": kv_lat_g[i], "w_ukv": w_ukv[i],
            "q_norm_g": q_norm_g[i], "k_norm_g": k_norm_g[i], "sg_norm_g": sg_norm_g[i], "sg_w": sg_w[i],
            "sg_b": sg_b[i], "w_br_pool": w_br_pool[i], "w_br_mla": w_br_mla[i], "w_br_sg": w_br_sg[i],
            "w_o": w_o[i], "norm2_g": norm2_g[i], "ffn_up": ffn_up[i], "ffn_conv_w": ffn_conv_w[i],
            "ffn_conv_b": ffn_conv_b[i], "ffn_down": ffn_down[i],
        }
        w = _layer_weights(p)
        mod_lat = mod[i, :b]
        mod_ctx = jnp.broadcast_to(mod[i, b:b + 1], (b, 6, d))

        zl, gl = _in_proj(xl, p["norm1_g"], mod_lat, w["w_in"], tm=1024, tn=512)
        zc, gc = _in_proj(xc, p["norm1_g"], mod_ctx, w["w_in"], tm=1024, tn=512)
        kc, vc = _keys_values(zc, p, w, rope_ctx)
        kl, vl = _keys_values(zl, p, w, rope_lat)
        ql = _queries(zl, p, w, rope_lat)
        attn_l = _attention(ql, [(kl, vl), (kc, vc)], bq=512, bk=512)
        xl = _mix_and_ffn(xl, zl, gl, attn_l, p, w, mod_lat)
        if i < depth - 1:
            qc = _queries(zc, p, w, rope_ctx)
            attn_c = _attention(qc, [(kc, vc)], bq=512, bk=512)
            xc = _mix_and_ffn(xc, zc, gc, attn_c, p, w, mod_ctx)
    return xl
```

```python
import functools
import math

import jax
import jax.numpy as jnp
import numpy as np
from jax import lax
from jax.experimental import pallas as pl
from jax.experimental.pallas import tpu as pltpu

F32 = jnp.float32
BF16 = jnp.bfloat16

GRID_W = 64
EPS = 1e-6
POOL_W = 512
POOL_WINDOWS = (2, 4, 8, 16)
POOL_GROUP_W = POOL_W // len(POOL_WINDOWS)
N_HEADS = 8
Q_LORA = 512
KV_LORA = 256
QK_NOPE = 128
QK_ROPE = 64
QK_HEAD = QK_NOPE + QK_ROPE
V_HEAD = 128
ROPE_FREQS = QK_ROPE // 4
ROPE_THETA = 10000.0
SG_W = 512
SG_GROUPS = 4
SG_GROUP_W = SG_W // SG_GROUPS
CHUNK = 128
N_BRANCH = 3
CONV_W = 3

LANES = 128
SUBLANES = 8
BF16_ROWS = 16
HEAD_PAD = 2 * LANES
VMEM_LIMIT_BYTES = 56 * 1024 * 1024

ZS_POOL = 0
ZS_Q = 512
ZS_SG = 1024
ZS_KV = 2048
ZS_COLS = 2560
ROPE_SWAP = np.concatenate([np.arange(16, 32), np.arange(0, 16), np.arange(48, 64), np.arange(32, 48)])


def _cparams(*sem):
    return pltpu.CompilerParams(dimension_semantics=sem, vmem_limit_bytes=VMEM_LIMIT_BYTES)


def _rms_scale(x, width):
    return lax.rsqrt(jnp.sum(x * x, axis=-1, keepdims=True) * (1.0 / width) + EPS)


def _ada_kernel(c_ref, w_ref, b_ref, o_ref):
    c = c_ref[...]
    s = c * jax.nn.sigmoid(c)
    o_ref[...] = jnp.dot(s, w_ref[...], preferred_element_type=F32, precision=lax.Precision.HIGHEST) + b_ref[...]


def _ada(cvec, ada_w, ada_b):
    depth, d, n = ada_w.shape
    tn = 1024
    return pl.pallas_call(
        _ada_kernel,
        grid=(depth, n // tn),
        in_specs=[
            pl.BlockSpec((SUBLANES, d), lambda l, j: (0, 0)),
            pl.BlockSpec((None, d, tn), lambda l, j: (l, 0, j)),
            pl.BlockSpec((None, 1, tn), lambda l, j: (l, 0, j)),
        ],
        out_specs=pl.BlockSpec((None, SUBLANES, tn), lambda l, j: (l, 0, j)),
        out_shape=jax.ShapeDtypeStruct((depth, SUBLANES, n), F32),
        compiler_params=_cparams("parallel", "arbitrary"),
        name="ada_modulation",
    )(cvec, ada_w, ada_b.reshape(depth, 1, n))


def _norm_mod(x, g, mod_ref, shift_row, scale_row):
    y = x * _rms_scale(x, x.shape[-1]) * g
    return y * (1.0 + mod_ref[scale_row:scale_row + 1, :]) + mod_ref[shift_row:shift_row + 1, :]


def _sigmoid(x):
    return 0.5 * jnp.tanh(0.5 * x) + 0.5


def _norm_kernel(x_ref, g_ref, mod_ref, o_ref, *, shift_row, scale_row):
    o_ref[...] = _norm_mod(x_ref[...], g_ref[...], mod_ref, shift_row, scale_row).astype(o_ref.dtype)


def _norm_modulate(x, g, mod, *, shift_row, scale_row, tm):
    b, l, d = x.shape
    tm = min(tm, l)
    return pl.pallas_call(
        functools.partial(_norm_kernel, shift_row=shift_row, scale_row=scale_row),
        grid=(b, l // tm),
        in_specs=[
            pl.BlockSpec((None, tm, d), lambda bi, i: (bi, i, 0)),
            pl.BlockSpec((1, d), lambda bi, i: (0, 0)),
            pl.BlockSpec((None, 6, d), lambda bi, i: (bi, 0, 0)),
        ],
        out_specs=pl.BlockSpec((None, tm, d), lambda bi, i: (bi, i, 0)),
        out_shape=jax.ShapeDtypeStruct((b, l, d), BF16),
        compiler_params=_cparams("parallel", "parallel"),
        name="norm_modulate",
    )(x, g.reshape(1, d), mod)


def _norm_halo_kernel(xp_ref, xc_ref, xn_ref, g_ref, mod_ref, o_ref, *, tm, nt, shift_row, scale_row):
    ti = pl.program_id(1)
    halo = BF16_ROWS
    g = g_ref[...]
    o_ref[halo:halo + tm, :] = _norm_mod(xc_ref[...], g, mod_ref, shift_row, scale_row).astype(o_ref.dtype)
    prev = jnp.where(ti > 0, _norm_mod(xp_ref[...], g, mod_ref, shift_row, scale_row), 0.0)
    o_ref[0:halo, :] = prev.astype(o_ref.dtype)
    nxt = jnp.where(ti < nt - 1, _norm_mod(xn_ref[...], g, mod_ref, shift_row, scale_row), 0.0)
    o_ref[halo + tm:, :] = nxt.astype(o_ref.dtype)


def _norm_modulate_tiles(x, g, mod, *, shift_row, scale_row, tm):
    b, l, d = x.shape
    tm = min(tm, l)
    nt = l // tm
    hb = tm // BF16_ROWS
    rows = tm + 2 * BF16_ROWS
    return pl.pallas_call(
        functools.partial(_norm_halo_kernel, tm=tm, nt=nt, shift_row=shift_row, scale_row=scale_row),
        grid=(b, nt),
        in_specs=[
            pl.BlockSpec((None, BF16_ROWS, d), lambda bi, i: (bi, jnp.maximum(i * hb - 1, 0), 0)),
            pl.BlockSpec((None, tm, d), lambda bi, i: (bi, i, 0)),
            pl.BlockSpec((None, BF16_ROWS, d), lambda bi, i: (bi, jnp.minimum((i + 1) * hb, l // BF16_ROWS - 1), 0)),
            pl.BlockSpec((1, d), lambda bi, i: (0, 0)),
            pl.BlockSpec((None, 6, d), lambda bi, i: (bi, 0, 0)),
        ],
        out_specs=pl.BlockSpec((None, None, rows, d), lambda bi, i: (bi, i, 0, 0)),
        out_shape=jax.ShapeDtypeStruct((b, nt, rows, d), BF16),
        compiler_params=_cparams("parallel", "parallel"),
        name="norm_modulate_tiles",
    )(x, x, x, g.reshape(1, d), mod)


def _matmul_kernel(a_ref, w_ref, o_ref, *, sigmoid):
    z = jnp.dot(a_ref[...], w_ref[...], preferred_element_type=F32)
    if sigmoid:
        z = _sigmoid(z)
    o_ref[...] = z.astype(o_ref.dtype)


def _matmul(a, w, *, sigmoid, out_dtype, tm, tn, name):
    b, l, k = a.shape
    n = w.shape[1]
    tm = min(tm, l)
    return pl.pallas_call(
        functools.partial(_matmul_kernel, sigmoid=sigmoid),
        grid=(b, l // tm, n // tn),
        in_specs=[
            pl.BlockSpec((None, tm, k), lambda bi, i, j: (bi, i, 0)),
            pl.BlockSpec((k, tn), lambda bi, i, j: (0, j)),
        ],
        out_specs=pl.BlockSpec((None, tm, tn), lambda bi, i, j: (bi, i, j)),
        out_shape=jax.ShapeDtypeStruct((b, l, n), out_dtype),
        compiler_params=_cparams("parallel", "parallel", "arbitrary"),
        name=name,
    )(a, w)


def _pool_kernel(prev_ref, cur_ref, next_ref, pw_ref, ps_ref, o_ref, ext_ref, *, tm, nt, seq_len):
    ti = pl.program_id(1)
    cur = cur_ref[...]
    ext_ref[SUBLANES:SUBLANES + tm, :] = cur
    ext_ref[0:SUBLANES, :] = jnp.where(ti > 0, prev_ref[...], 0.0)
    ext_ref[SUBLANES + tm:, :] = jnp.where(ti < nt - 1, next_ref[...], 0.0)
    t = ti * tm + lax.broadcasted_iota(jnp.int32, (tm, 1), 0)
    for g, w in enumerate(POOL_WINDOWS):
        r = w // 2
        c0 = g * POOL_GROUP_W
        s = ext_ref[SUBLANES - r:SUBLANES - r + tm, c0:c0 + POOL_GROUP_W]
        for dlt in range(-r + 1, r):
            s = s + ext_ref[SUBLANES + dlt:SUBLANES + dlt + tm, c0:c0 + POOL_GROUP_W]
        cnt = (jnp.minimum(t + r, seq_len) - jnp.maximum(t - r, 0)).astype(F32)
        pooled = s / cnt - cur[:, c0:c0 + POOL_GROUP_W]
        y = jnp.dot(pooled.astype(BF16), pw_ref[g], preferred_element_type=F32)
        o_ref[:, c0:c0 + POOL_GROUP_W] = (y * ps_ref[:, c0:c0 + POOL_GROUP_W]).astype(o_ref.dtype)


def _pool_mix(zs, pool_w, pool_scale, *, tm):
    b, l, _ = zs.shape
    tm = min(tm, l)
    nt = l // tm
    hb = tm // SUBLANES
    return pl.pallas_call(
        functools.partial(_pool_kernel, tm=tm, nt=nt, seq_len=l),
        grid=(b, nt),
        in_specs=[
            pl.BlockSpec((None, SUBLANES, POOL_W), lambda bi, i: (bi, jnp.maximum(i * hb - 1, 0), 0)),
            pl.BlockSpec((None, tm, POOL_W), lambda bi, i: (bi, i, 0)),
            pl.BlockSpec((None, SUBLANES, POOL_W), lambda bi, i: (bi, jnp.minimum((i + 1) * hb, l // SUBLANES - 1), 0)),
            pl.BlockSpec((len(POOL_WINDOWS), POOL_GROUP_W, POOL_GROUP_W), lambda bi, i: (0, 0, 0)),
            pl.BlockSpec((1, POOL_W), lambda bi, i: (0, 0)),
        ],
        out_specs=pl.BlockSpec((None, tm, POOL_W), lambda bi, i: (bi, i, 0)),
        out_shape=jax.ShapeDtypeStruct((b, l, POOL_W), BF16),
        scratch_shapes=[pltpu.VMEM((tm + 2 * SUBLANES, POOL_W), F32)],
        compiler_params=_cparams("parallel", "parallel"),
        name="pool_mix",
    )(zs, zs, zs, pool_w, pool_scale.reshape(1, POOL_W))


def _sg_kernel(z_ref, g_ref, w_ref, b_ref, o_ref, *, tm):
    z = z_ref[...]
    a = z * (0.5 * (1.0 + jnp.tanh(math.sqrt(2.0 / math.pi) * (z + 0.044715 * (z * z * z)))))
    u = a[:, :SG_W]
    v = a[:, SG_W:]
    vc = v - jnp.mean(v, axis=-1, keepdims=True)
    vn = (vc * lax.rsqrt(jnp.mean(vc * vc, axis=-1, keepdims=True) + EPS) * g_ref[...]).astype(BF16)
    for c in range(tm // CHUNK):
        rows = slice(c * CHUNK, (c + 1) * CHUNK)
        for g in range(SG_GROUPS):
            cols = slice(g * SG_GROUP_W, (g + 1) * SG_GROUP_W)
            s = jnp.dot(w_ref[g], vn[rows, cols], preferred_element_type=F32) + b_ref[g]
            o_ref[rows, cols] = (u[rows, cols] * s).astype(o_ref.dtype)


def _spatial_gating(zs, sg_norm_g, sg_w, sg_b, *, tm):
    b, l, _ = zs.shape
    tm = min(tm, l)
    return pl.pallas_call(
        functools.partial(_sg_kernel, tm=tm),
        grid=(b, l // tm),
        in_specs=[
            pl.BlockSpec((None, tm, 2 * SG_W), lambda bi, i: (bi, i, ZS_SG // (2 * SG_W))),
            pl.BlockSpec((1, SG_W), lambda bi, i: (0, 0)),
            pl.BlockSpec((SG_GROUPS, CHUNK, CHUNK), lambda bi, i: (0, 0, 0)),
            pl.BlockSpec((SG_GROUPS, CHUNK, 1), lambda bi, i: (0, 0, 0)),
        ],
        out_specs=pl.BlockSpec((None, tm, SG_W), lambda bi, i: (bi, i, 0)),
        out_shape=jax.ShapeDtypeStruct((b, l, SG_W), BF16),
        compiler_params=_cparams("parallel", "parallel"),
        name="spatial_gating",
    )(zs, sg_norm_g.reshape(1, SG_W), sg_w, sg_b.reshape(SG_GROUPS, CHUNK, 1))


def _q_kernel(z_ref, gl_ref, w_ref, gn_ref, gr_ref, grs_ref, cos_ref, sin_ref, o_ref, *, out_scale):
    z = z_ref[...]
    zn = (z * _rms_scale(z, Q_LORA) * gl_ref[...]).astype(BF16)
    q = jnp.dot(zn, w_ref[...], preferred_element_type=F32)
    nope_w = N_HEADS * QK_NOPE
    rope_w = N_HEADS * QK_ROPE
    lo = lax.broadcasted_iota(jnp.int32, (1, LANES), 1) < QK_ROPE
    cos = cos_ref[...]
    sin = sin_ref[...]
    gn = gn_ref[...] * out_scale
    for pair in range(N_HEADS // 2):
        blk = slice(pair * LANES, (pair + 1) * LANES)
        r2 = q[:, nope_w + pair * LANES:nope_w + (pair + 1) * LANES]
        r2s = q[:, nope_w + rope_w + pair * LANES:nope_w + rope_w + (pair + 1) * LANES]
        sq = r2 * r2
        ss_lo = jnp.sum(jnp.where(lo, sq, 0.0), axis=-1, keepdims=True)
        ss_hi = jnp.sum(jnp.where(lo, 0.0, sq), axis=-1, keepdims=True)
        rot = r2 * gr_ref[:, blk] * cos + r2s * grs_ref[:, blk] * sin
        for half, ss_r in enumerate((ss_lo, ss_hi)):
            h = 2 * pair + half
            nope = q[:, h * QK_NOPE:(h + 1) * QK_NOPE]
            inv = lax.rsqrt((jnp.sum(nope * nope, axis=-1, keepdims=True) + ss_r) * (1.0 / QK_HEAD) + EPS)
            o_ref[h, :, 0:LANES] = (nope * inv * gn).astype(o_ref.dtype)
            keep = lo if half == 0 else jnp.logical_not(lo)
            o_ref[h, :, LANES:HEAD_PAD] = jnp.where(keep, rot * (inv * out_scale), 0.0).astype(o_ref.dtype)


def _q_proj(zs, q_lat_g, w_q, gn, gr, grs, cos_tab, sin_tab, *, out_scale, tm):
    b, l, _ = zs.shape
    tm = min(tm, l)
    return pl.pallas_call(
        functools.partial(_q_kernel, out_scale=out_scale),
        grid=(b, l // tm),
        in_specs=[
            pl.BlockSpec((None, tm, Q_LORA), lambda bi, i: (bi, i, ZS_Q // Q_LORA)),
            pl.BlockSpec((1, Q_LORA), lambda bi, i: (0, 0)),
            pl.BlockSpec(w_q.shape, lambda bi, i: (0, 0)),
            pl.BlockSpec((1, QK_NOPE), lambda bi, i: (0, 0)),
            pl.BlockSpec((1, N_HEADS * QK_ROPE), lambda bi, i: (0, 0)),
            pl.BlockSpec((1, N_HEADS * QK_ROPE), lambda bi, i: (0, 0)),
            pl.BlockSpec((tm, LANES), lambda bi, i: (i, 0)),
            pl.BlockSpec((tm, LANES), lambda bi, i: (i, 0)),
        ],
        out_specs=pl.BlockSpec((None, N_HEADS, tm, HEAD_PAD), lambda bi, i: (bi, 0, i, 0)),
        out_shape=jax.ShapeDtypeStruct((b, N_HEADS, l, HEAD_PAD), BF16),
        compiler_params=_cparams("parallel", "parallel"),
        name="q_proj",
    )(zs, q_lat_g.reshape(1, Q_LORA), w_q, gn, gr, grs, cos_tab, sin_tab)


def _kv_kernel(z_ref, gl_ref, w_ref, gn_ref, gt_ref, tab_ref, k_ref, v_ref):
    z = z_ref[...]
    zkv = z[:, :KV_LORA]
    zn = (zkv * _rms_scale(zkv, KV_LORA) * gl_ref[...]).astype(BF16)
    kv = jnp.dot(zn, w_ref[...], preferred_element_type=F32)
    lo = lax.broadcasted_iota(jnp.int32, (1, LANES), 1) < QK_ROPE
    kr2 = z[:, KV_LORA:KV_LORA + LANES]
    ss_r = jnp.sum(jnp.where(lo, kr2 * kr2, 0.0), axis=-1, keepdims=True)
    t = kr2 * gt_ref[...] * tab_ref[...]
    rot = t + pltpu.roll(t, QK_ROPE, axis=1)
    gn = gn_ref[...]
    ones = jnp.ones((z.shape[0], LANES), v_ref.dtype)
    for h in range(N_HEADS):
        nope = kv[:, h * QK_NOPE:(h + 1) * QK_NOPE]
        inv = lax.rsqrt((jnp.sum(nope * nope, axis=-1, keepdims=True) + ss_r) * (1.0 / QK_HEAD) + EPS)
        k_ref[h, :, 0:LANES] = (nope * inv * gn).astype(k_ref.dtype)
        keep = lo if h % 2 == 0 else jnp.logical_not(lo)
        k_ref[h, :, LANES:HEAD_PAD] = jnp.where(keep, rot * inv, 0.0).astype(k_ref.dtype)
        v_ref[h, :, 0:LANES] = kv[:, N_HEADS * QK_NOPE + h * V_HEAD:N_HEADS * QK_NOPE + (h + 1) * V_HEAD].astype(
            v_ref.dtype)
        v_ref[h, :, LANES:HEAD_PAD] = ones


def _kv_proj(zs, kv_lat_g, w_kv, gn, gt, tab, *, tm):
    b, l, _ = zs.shape
    tm = min(tm, l)
    blk = KV_LORA + 2 * LANES
    shape = jax.ShapeDtypeStruct((b, N_HEADS, l, HEAD_PAD), BF16)
    spec = pl.BlockSpec((None, N_HEADS, tm, HEAD_PAD), lambda bi, i: (bi, 0, i, 0))
    return pl.pallas_call(
        _kv_kernel,
        grid=(b, l // tm),
        in_specs=[
            pl.BlockSpec((None, tm, blk), lambda bi, i: (bi, i, ZS_KV // blk)),
            pl.BlockSpec((1, KV_LORA), lambda bi, i: (0, 0)),
            pl.BlockSpec(w_kv.shape, lambda bi, i: (0, 0)),
            pl.BlockSpec((1, QK_NOPE), lambda bi, i: (0, 0)),
            pl.BlockSpec((1, LANES), lambda bi, i: (0, 0)),
            pl.BlockSpec((tm, LANES), lambda bi, i: (i, 0)),
        ],
        out_specs=[spec, spec],
        out_shape=[shape, shape],
        compiler_params=_cparams("parallel", "parallel"),
        name="kv_proj",
    )(zs, kv_lat_g.reshape(1, KV_LORA), w_kv, gn, gt, tab)


def _attn_kernel(q_ref, *refs, chunks):
    n_src = len(chunks)
    o_ref, acc_ref, m_ref = refs[2 * n_src:]
    q = q_ref[...]
    acc_ref[...] = jnp.zeros_like(acc_ref)
    m_ref[...] = jnp.full_like(m_ref, -jnp.inf)

    def step(k, v):
        s = lax.dot_general(q, k, (((1,), (1,)), ((), ())), preferred_element_type=F32)
        m_prev = m_ref[...]
        m_new = jnp.maximum(m_prev, jnp.max(s, axis=-1, keepdims=True))
        alpha = jnp.exp2(m_prev - m_new)
        p = jnp.exp2(s - jnp.concatenate([m_new] * (s.shape[1] // LANES), axis=1)).astype(BF16)
        pv = jnp.dot(p, v, preferred_element_type=F32)
        acc_ref[...] = acc_ref[...] * jnp.concatenate([alpha, alpha], axis=1) + pv
        m_ref[...] = m_new

    for si, (bk, n) in enumerate(chunks):
        k_ref, v_ref = refs[2 * si], refs[2 * si + 1]
        if n == 1:
            step(k_ref[...], v_ref[...])
        else:
            def body(c, carry, k_ref=k_ref, v_ref=v_ref, bk=bk):
                off = pl.multiple_of(c * bk, bk)
                step(k_ref[pl.ds(off, bk), :], v_ref[pl.ds(off, bk), :])
                return carry

            lax.fori_loop(0, n, body, 0, unroll=16)
    acc = acc_ref[...]
    o_ref[...] = (acc[:, :V_HEAD] / acc[:, V_HEAD:]).astype(o_ref.dtype)


def _attention(q, sources, *, bq, bk):
    b, h, l, _ = q.shape
    bq = min(bq, l)
    chunks = []
    in_specs = [pl.BlockSpec((None, None, bq, HEAD_PAD), lambda bi, hi, i: (bi, hi, i, 0))]
    args = [q]
    for k, v in sources:
        lk = k.shape[2]
        cb = min(bk, lk)
        chunks.append((cb, lk // cb))
        for a in (k, v):
            in_specs.append(pl.BlockSpec((None, None, lk, HEAD_PAD), lambda bi, hi, i: (bi, hi, 0, 0)))
            args.append(a)
    return pl.pallas_call(
        functools.partial(_attn_kernel, chunks=tuple(chunks)),
        grid=(b, h, l // bq),
        in_specs=in_specs,
        out_specs=pl.BlockSpec((None, bq, V_HEAD), lambda bi, hi, i: (bi, i, hi)),
        out_shape=jax.ShapeDtypeStruct((b, l, h * V_HEAD), BF16),
        scratch_shapes=[pltpu.VMEM((bq, HEAD_PAD), F32), pltpu.VMEM((bq, LANES), F32)],
        compiler_params=_cparams("parallel", "parallel", "arbitrary"),
        name="latent_attention",
    )(*args)


def _merge_out_kernel(p_ref, a_ref, s_ref, gp_ref, ga_ref, gs_ref, wp_ref, wa_ref, ws_ref, wo_ref, x_ref, g1_ref,
                      o_ref, y_ref, *, nj, tn):
    j = pl.program_id(2)

    @pl.when(j < nj)
    def _():
        y = gp_ref[...].astype(F32) * jnp.dot(p_ref[...], wp_ref[...], preferred_element_type=F32)
        y += ga_ref[...].astype(F32) * jnp.dot(a_ref[...], wa_ref[...], preferred_element_type=F32)
        y += gs_ref[...].astype(F32) * jnp.dot(s_ref[...], ws_ref[...], preferred_element_type=F32)
        y_ref[j] = y.astype(y_ref.dtype)

    @pl.when(j >= nj)
    def _():
        acc = jnp.dot(y_ref[0], wo_ref[0:tn, :], preferred_element_type=F32)
        for c in range(1, nj):
            acc += jnp.dot(y_ref[c], wo_ref[c * tn:(c + 1) * tn, :], preferred_element_type=F32)
        o_ref[...] = x_ref[...] + g1_ref[...] * acc


def _merge_out(pool_out, attn, sg_out, gates, w_p, w_a, w_s, w_o, x, mod, *, tm, tn):
    b, l, _ = pool_out.shape
    d = w_p.shape[1]
    tm = min(tm, l)
    nj = d // tn

    def first(j):
        return jnp.minimum(j, nj - 1)

    def second(j):
        return jnp.maximum(j - nj, 0)

    def act(width):
        return pl.BlockSpec((None, tm, width), lambda bi, i, j: (bi, i, 0))

    def gate(branch):
        return pl.BlockSpec((None, tm, tn), lambda bi, i, j: (bi, i, branch * nj + first(j)))

    def wgt(width):
        return pl.BlockSpec((width, tn), lambda bi, i, j: (0, first(j)))

    return pl.pallas_call(
        functools.partial(_merge_out_kernel, nj=nj, tn=tn),
        grid=(b, l // tm, 2 * nj),
        in_specs=[act(POOL_W), act(N_HEADS * V_HEAD), act(SG_W), gate(0), gate(1), gate(2),
                  wgt(POOL_W), wgt(N_HEADS * V_HEAD), wgt(SG_W),
                  pl.BlockSpec((d, tn), lambda bi, i, j: (0, second(j))),
                  pl.BlockSpec((None, tm, tn), lambda bi, i, j: (bi, i, second(j))),
                  pl.BlockSpec((None, 1, tn), lambda bi, i, j: (bi, 0, second(j)))],
        out_specs=pl.BlockSpec((None, tm, tn), lambda bi, i, j: (bi, i, second(j))),
        out_shape=jax.ShapeDtypeStruct((b, l, d), F32),
        scratch_shapes=[pltpu.VMEM((nj, tm, tn), BF16)],
        compiler_params=_cparams("parallel", "parallel", "arbitrary"),
        name="merge_out_proj",
    )(pool_out, attn, sg_out, gates, gates, gates, w_p, w_a, w_s, w_o, x, mod[:, 2:3, :])


def _mm_res_kernel(a_ref, w_ref, x_ref, gate_ref, o_ref):
    y = jnp.dot(a_ref[...], w_ref[...], preferred_element_type=F32)
    o_ref[...] = x_ref[...] + gate_ref[...] * y


def _mm_residual(a, w, x, mod, gate_row, *, tm, tn, name):
    b, l, k = a.shape
    d = w.shape[1]
    tm = min(tm, l)
    return pl.pallas_call(
        _mm_res_kernel,
        grid=(b, l // tm, d // tn),
        in_specs=[
            pl.BlockSpec((None, tm, k), lambda bi, i, j: (bi, i, 0)),
            pl.BlockSpec((k, tn), lambda bi, i, j: (0, j)),
            pl.BlockSpec((None, tm, tn), lambda bi, i, j: (bi, i, j)),
            pl.BlockSpec((None, 1, tn), lambda bi, i, j: (bi, 0, j)),
        ],
        out_specs=pl.BlockSpec((None, tm, tn), lambda bi, i, j: (bi, i, j)),
        out_shape=jax.ShapeDtypeStruct((b, l, d), F32),
        compiler_params=_cparams("parallel", "parallel", "arbitrary"),
        name=name,
    )(a, w, x, mod[:, gate_row:gate_row + 1, :])


def _ffn_up_kernel(h_ref, wa_ref, wv_ref, cwa_ref, cwv_ref, cba_ref, cbv_ref, o_ref, *, tm):
    halo = BF16_ROWS
    h = h_ref[...]
    rows = h.shape[0]

    def conv(w_ref, cw_ref, cb_ref):
        u = jnp.dot(h, w_ref[...], preferred_element_type=F32)
        acc = cb_ref[...] + pltpu.roll(u, 1, axis=0) * cw_ref[0:1, :]
        acc = acc + u * cw_ref[1:2, :]
        acc = acc + pltpu.roll(u, rows - 1, axis=0) * cw_ref[2:3, :]
        return acc[halo:halo + tm, :]

    a = conv(wa_ref, cwa_ref, cba_ref)
    v = conv(wv_ref, cwv_ref, cbv_ref)
    o_ref[...] = (a * _sigmoid(a) * v).astype(o_ref.dtype)


def _ffn_up(h_tiles, w_up, conv_w, conv_b, *, tf):
    b, nt, rows, d = h_tiles.shape
    tm = rows - 2 * BF16_ROWS
    l = nt * tm
    d_ff = w_up.shape[1] // 2
    nf = d_ff // tf
    return pl.pallas_call(
        functools.partial(_ffn_up_kernel, tm=tm),
        grid=(b, nt, nf),
        in_specs=[
            pl.BlockSpec((None, None, rows, d), lambda bi, i, j: (bi, i, 0, 0)),
            pl.BlockSpec((d, tf), lambda bi, i, j: (0, j)),
            pl.BlockSpec((d, tf), lambda bi, i, j: (0, nf + j)),
            pl.BlockSpec((CONV_W, tf), lambda bi, i, j: (0, j)),
            pl.BlockSpec((CONV_W, tf), lambda bi, i, j: (0, nf + j)),
            pl.BlockSpec((1, tf), lambda bi, i, j: (0, j)),
            pl.BlockSpec((1, tf), lambda bi, i, j: (0, nf + j)),
        ],
        out_specs=pl.BlockSpec((None, tm, tf), lambda bi, i, j: (bi, i, j)),
        out_shape=jax.ShapeDtypeStruct((b, l, d_ff), BF16),
        compiler_params=_cparams("parallel", "parallel", "arbitrary"),
        name="ffn_up_conv_gate",
    )(h_tiles, w_up, w_up, conv_w, conv_w, conv_b.reshape(1, -1), conv_b.reshape(1, -1))


def _rope_tables(n_tokens):
    t = jnp.arange(n_tokens)
    row = (t // GRID_W).astype(F32)
    col = (t % GRID_W).astype(F32)
    inv_freq = ROPE_THETA ** (-jnp.arange(ROPE_FREQS, dtype=F32) / ROPE_FREQS)
    ar = row[:, None] * inv_freq
    ac = col[:, None] * inv_freq
    c = jnp.concatenate([jnp.cos(ar), jnp.cos(ar), jnp.cos(ac), jnp.cos(ac)], axis=1)
    s = jnp.concatenate([-jnp.sin(ar), jnp.sin(ar), -jnp.sin(ac), jnp.sin(ac)], axis=1)
    return c, s


def _identity_tables(n_tokens):
    return jnp.ones((n_tokens, QK_ROPE), F32), jnp.zeros((n_tokens, QK_ROPE), F32)


def _layer_weights(p):
    d = p["w_in"].shape[0]
    w_in = p["w_in"]
    o_q, o_kv, o_kr, o_sg, o_gate = POOL_W, POOL_W + Q_LORA, POOL_W + Q_LORA + KV_LORA, \
        POOL_W + Q_LORA + KV_LORA + QK_ROPE, POOL_W + Q_LORA + KV_LORA + QK_ROPE + 2 * SG_W
    w_kr = w_in[:, o_kr:o_sg]
    w_small = jnp.concatenate([
        w_in[:, :o_q], w_in[:, o_q:o_kv], w_in[:, o_sg:o_gate], w_in[:, o_kv:o_kr], w_kr, w_kr[:, ROPE_SWAP],
        jnp.zeros((d, ZS_COLS - ZS_KV - KV_LORA - 2 * QK_ROPE), w_in.dtype)], axis=1).astype(BF16)
    w_gate = w_in[:, o_gate:].astype(BF16)

    w_uq = p["w_uq"].reshape(Q_LORA, N_HEADS, QK_HEAD)
    w_q_rope = w_uq[:, :, QK_NOPE:]
    w_q = jnp.concatenate([
        w_uq[:, :, :QK_NOPE].reshape(Q_LORA, -1), w_q_rope.reshape(Q_LORA, -1),
        w_q_rope[:, :, ROPE_SWAP].reshape(Q_LORA, -1)], axis=1).astype(BF16)
    gq = p["q_norm_g"]
    gq_n = gq[:QK_NOPE].reshape(1, QK_NOPE)
    gq_r = jnp.tile(gq[QK_NOPE:], N_HEADS).reshape(1, -1)
    gq_rs = jnp.tile(gq[QK_NOPE:][ROPE_SWAP], N_HEADS).reshape(1, -1)

    w_ukv = p["w_ukv"].reshape(KV_LORA, N_HEADS, QK_NOPE + V_HEAD)
    w_kv = jnp.concatenate([
        w_ukv[:, :, :QK_NOPE].reshape(KV_LORA, -1), w_ukv[:, :, QK_NOPE:].reshape(KV_LORA, -1)], axis=1).astype(BF16)
    gk = p["k_norm_g"]
    gk_n = gk[:QK_NOPE].reshape(1, QK_NOPE)
    gk_t = jnp.concatenate([gk[QK_NOPE:], gk[QK_NOPE:][ROPE_SWAP]]).reshape(1, LANES)
    return dict(
        w_small=w_small, w_gate=w_gate, w_q=w_q, gq_n=gq_n, gq_r=gq_r, gq_rs=gq_rs, w_kv=w_kv, gk_n=gk_n, gk_t=gk_t,
        pool_w=p["pool_w"].astype(BF16), sg_w=p["sg_w"].astype(BF16),
        w_br_pool=p["w_br_pool"].astype(BF16), w_br_mla=p["w_br_mla"].astype(BF16),
        w_br_sg=p["w_br_sg"].astype(BF16), w_o=p["w_o"].astype(BF16),
        ffn_up=p["ffn_up"].astype(BF16), ffn_down=p["ffn_down"].astype(BF16))


def _keys_values(zs, p, w, tabs):
    c, s = tabs
    return _kv_proj(zs, p["kv_lat_g"], w["w_kv"], w["gk_n"], w["gk_t"], jnp.concatenate([c, s], axis=1), tm=512)


def _queries(zs, p, w, tabs):
    c, s = tabs
    out_scale = QK_HEAD ** -0.5 * math.log2(math.e)
    return _q_proj(zs, p["q_lat_g"], w["w_q"], w["gq_n"], w["gq_r"], w["gq_rs"], jnp.concatenate([c, c], axis=1),
                   jnp.concatenate([s, s], axis=1), out_scale=out_scale, tm=512)


def _in_proj(x, p, w, mod, *, gates):
    h = _norm_modulate(x, p["norm1_g"], mod, shift_row=0, scale_row=1, tm=1024)
    zs = _matmul(h, w["w_small"], sigmoid=False, out_dtype=F32, tm=1024, tn=512, name="in_proj_small")
    g = _matmul(h, w["w_gate"], sigmoid=True, out_dtype=BF16, tm=1024, tn=512, name="in_proj_gate") if gates else None
    return zs, g


def _mix_and_ffn(x, zs, gates, attn, p, w, mod):
    pool_out = _pool_mix(zs, w["pool_w"], p["pool_scale"], tm=512)
    sg_out = _spatial_gating(zs, p["sg_norm_g"], w["sg_w"], p["sg_b"], tm=512)
    x = _merge_out(pool_out, attn, sg_out, gates, w["w_br_pool"], w["w_br_mla"], w["w_br_sg"], w["w_o"], x, mod,
                   tm=1024, tn=512)
    h2 = _norm_modulate_tiles(x, p["norm2_g"], mod, shift_row=3, scale_row=4, tm=1024)
    act = _ffn_up(h2, w["ffn_up"], p["ffn_conv_w"], p["ffn_conv_b"], tf=512)
    return _mm_residual(act, w["ffn_down"], x, mod, 5, tm=1024, tn=512, name="ffn_down_residual")


def kernel(x, c, ctx, c_ctx, ada_w, ada_b, norm1_g, w_in, pool_w, pool_scale, q_lat_g, w_uq, kv_lat_g, w_ukv, q_norm_g, k_norm_g, sg_norm_g, sg_w, sg_b, w_br_pool, w_br_mla, w_br_sg, w_o, norm2_g, ffn_up, ffn_conv_w, ffn_conv_b, ffn_down):
    b, l, d = x.shape
    lc = ctx.shape[1]
    depth = ada_w.shape[0]
    assert b + 1 <= SUBLANES

    cvec = jnp.concatenate([c, c_ctx[None, :], jnp.zeros((SUBLANES - b - 1, d), c.dtype)], axis=0)
    mod = _ada(cvec, ada_w, ada_b).reshape(depth, SUBLANES, 6, d)
    rope_lat = _rope_tables(l)
    rope_ctx = _identity_tables(lc)

    xl, xc = x, ctx
    for i in range(depth):
        p = {
            "norm1_g": norm1_g[i], "w_in": w_in[i], "pool_w": pool_w[i], "pool_scale": pool_scale[i],
            "q_lat_g": q_lat_g[i], "w_uq": w_uq[i], "kv_lat_g": kv_lat_g[i], "w_ukv": w_ukv[i],
            "q_norm_g": q_norm_g[i], "k_norm_g": k_norm_g[i], "sg_norm_g": sg_norm_g[i], "sg_w": sg_w[i],
            "sg_b": sg_b[i], "w_br_pool": w_br_pool[i], "w_br_mla": w_br_mla[i], "w_br_sg": w_br_sg[i],
            "w_o": w_o[i], "norm2_g": norm2_g[i], "ffn_up": ffn_up[i], "ffn_conv_w": ffn_conv_w[i],
            "ffn_conv_b": ffn_conv_b[i], "ffn_down": ffn_down[i],
        }
        w = _layer_weights(p)
        mod_lat = mod[i, :b]
        mod_ctx = jnp.broadcast_to(mod[i, b:b + 1], (b, 6, d))

        zl, gl = _in_proj(xl, p, w, mod_lat, gates=True)
        zc, gc = _in_proj(xc, p, w, mod_ctx, gates=i < depth - 1)
        kc, vc = _keys_values(zc, p, w, rope_ctx)
        kl, vl = _keys_values(zl, p, w, rope_lat)
        ql = _queries(zl, p, w, rope_lat)
        attn_l = _attention(ql, [(kl, vl), (kc, vc)], bq=1024, bk=512)
        xl = _mix_and_ffn(xl, zl, gl, attn_l, p, w, mod_lat)
        if i < depth - 1:
            qc = _queries(zc, p, w, rope_ctx)
            attn_c = _attention(qc, [(kc, vc)], bq=512, bk=512)
            xc = _mix_and_ffn(xc, zc, gc, attn_c, p, w, mod_ctx)
    return xl
```

```python
import functools
import math

import jax
import jax.numpy as jnp
import numpy as np
from jax import lax
from jax.experimental import pallas as pl
from jax.experimental.pallas import tpu as pltpu

F32 = jnp.float32
BF16 = jnp.bfloat16

GRID_W = 64
EPS = 1e-6
POOL_W = 512
POOL_WINDOWS = (2, 4, 8, 16)
POOL_GROUP_W = POOL_W // len(POOL_WINDOWS)
N_HEADS = 8
Q_LORA = 512
KV_LORA = 256
QK_NOPE = 128
QK_ROPE = 64
QK_HEAD = QK_NOPE + QK_ROPE
V_HEAD = 128
ROPE_FREQS = QK_ROPE // 4
ROPE_THETA = 10000.0
SG_W = 512
SG_GROUPS = 4
SG_GROUP_W = SG_W // SG_GROUPS
CHUNK = 128
N_BRANCH = 3
CONV_W = 3

LANES = 128
SUBLANES = 8
BF16_ROWS = 16
HEAD_PAD = 2 * LANES
VMEM_LIMIT_BYTES = 56 * 1024 * 1024

ZS_POOL = 0
ZS_Q = 512
ZS_SG = 1024
ZS_KV = 2048
ZS_COLS = 2560
ROPE_SWAP = np.concatenate([np.arange(16, 32), np.arange(0, 16), np.arange(48, 64), np.arange(32, 48)])


def _cparams(*sem):
    return pltpu.CompilerParams(dimension_semantics=sem, vmem_limit_bytes=VMEM_LIMIT_BYTES)


def _rms_scale(x, width):
    return lax.rsqrt(jnp.sum(x * x, axis=-1, keepdims=True) * (1.0 / width) + EPS)


def _ada_kernel(c_ref, w_ref, b_ref, o_ref):
    c = c_ref[...]
    s = c * jax.nn.sigmoid(c)
    o_ref[...] = jnp.dot(s, w_ref[...], preferred_element_type=F32, precision=lax.Precision.HIGHEST) + b_ref[...]


def _ada(cvec, ada_w, ada_b):
    depth, d, n = ada_w.shape
    tn = 1024
    return pl.pallas_call(
        _ada_kernel,
        grid=(depth, n // tn),
        in_specs=[
            pl.BlockSpec((SUBLANES, d), lambda l, j: (0, 0)),
            pl.BlockSpec((None, d, tn), lambda l, j: (l, 0, j)),
            pl.BlockSpec((None, 1, tn), lambda l, j: (l, 0, j)),
        ],
        out_specs=pl.BlockSpec((None, SUBLANES, tn), lambda l, j: (l, 0, j)),
        out_shape=jax.ShapeDtypeStruct((depth, SUBLANES, n), F32),
        compiler_params=_cparams("parallel", "arbitrary"),
        name="ada_modulation",
    )(cvec, ada_w, ada_b.reshape(depth, 1, n))


def _norm_mod(x, g, mod_ref, shift_row, scale_row):
    y = x * _rms_scale(x, x.shape[-1]) * g
    return y * (1.0 + mod_ref[scale_row:scale_row + 1, :]) + mod_ref[shift_row:shift_row + 1, :]


def _sigmoid(x):
    return 0.5 * jnp.tanh(0.5 * x) + 0.5


def _norm_kernel(x_ref, g_ref, mod_ref, o_ref, *, shift_row, scale_row):
    o_ref[...] = _norm_mod(x_ref[...], g_ref[...], mod_ref, shift_row, scale_row).astype(o_ref.dtype)


def _norm_modulate(x, g, mod, *, shift_row, scale_row, tm):
    b, l, d = x.shape
    tm = min(tm, l)
    return pl.pallas_call(
        functools.partial(_norm_kernel, shift_row=shift_row, scale_row=scale_row),
        grid=(b, l // tm),
        in_specs=[
            pl.BlockSpec((None, tm, d), lambda bi, i: (bi, i, 0)),
            pl.BlockSpec((1, d), lambda bi, i: (0, 0)),
            pl.BlockSpec((None, 6, d), lambda bi, i: (bi, 0, 0)),
        ],
        out_specs=pl.BlockSpec((None, tm, d), lambda bi, i: (bi, i, 0)),
        out_shape=jax.ShapeDtypeStruct((b, l, d), BF16),
        compiler_params=_cparams("parallel", "parallel"),
        name="norm_modulate",
    )(x, g.reshape(1, d), mod)


def _norm_halo_kernel(xp_ref, xc_ref, xn_ref, g_ref, mod_ref, o_ref, *, tm, nt, shift_row, scale_row):
    ti = pl.program_id(1)
    halo = BF16_ROWS
    g = g_ref[...]
    o_ref[halo:halo + tm, :] = _norm_mod(xc_ref[...], g, mod_ref, shift_row, scale_row).astype(o_ref.dtype)
    prev = jnp.where(ti > 0, _norm_mod(xp_ref[...], g, mod_ref, shift_row, scale_row), 0.0)
    o_ref[0:halo, :] = prev.astype(o_ref.dtype)
    nxt = jnp.where(ti < nt - 1, _norm_mod(xn_ref[...], g, mod_ref, shift_row, scale_row), 0.0)
    o_ref[halo + tm:, :] = nxt.astype(o_ref.dtype)


def _norm_modulate_tiles(x, g, mod, *, shift_row, scale_row, tm):
    b, l, d = x.shape
    tm = min(tm, l)
    nt = l // tm
    hb = tm // BF16_ROWS
    rows = tm + 2 * BF16_ROWS
    return pl.pallas_call(
        functools.partial(_norm_halo_kernel, tm=tm, nt=nt, shift_row=shift_row, scale_row=scale_row),
        grid=(b, nt),
        in_specs=[
            pl.BlockSpec((None, BF16_ROWS, d), lambda bi, i: (bi, jnp.maximum(i * hb - 1, 0), 0)),
            pl.BlockSpec((None, tm, d), lambda bi, i: (bi, i, 0)),
            pl.BlockSpec((None, BF16_ROWS, d), lambda bi, i: (bi, jnp.minimum((i + 1) * hb, l // BF16_ROWS - 1), 0)),
            pl.BlockSpec((1, d), lambda bi, i: (0, 0)),
            pl.BlockSpec((None, 6, d), lambda bi, i: (bi, 0, 0)),
        ],
        out_specs=pl.BlockSpec((None, None, rows, d), lambda bi, i: (bi, i, 0, 0)),
        out_shape=jax.ShapeDtypeStruct((b, nt, rows, d), BF16),
        compiler_params=_cparams("parallel", "parallel"),
        name="norm_modulate_tiles",
    )(x, x, x, g.reshape(1, d), mod)


def _matmul_kernel(a_ref, w_ref, o_ref, *, sigmoid):
    z = jnp.dot(a_ref[...], w_ref[...], preferred_element_type=F32)
    if sigmoid:
        z = _sigmoid(z)
    o_ref[...] = z.astype(o_ref.dtype)


def _matmul(a, w, layer, *, sigmoid, out_dtype, tm, tn, name):
    b, l, k = a.shape
    n = w.shape[2]
    tm = min(tm, l)
    return pl.pallas_call(
        functools.partial(_matmul_kernel, sigmoid=sigmoid),
        grid=(b, l // tm, n // tn),
        in_specs=[
            pl.BlockSpec((None, tm, k), lambda bi, i, j: (bi, i, 0)),
            pl.BlockSpec((None, k, tn), lambda bi, i, j: (layer, 0, j)),
        ],
        out_specs=pl.BlockSpec((None, tm, tn), lambda bi, i, j: (bi, i, j)),
        out_shape=jax.ShapeDtypeStruct((b, l, n), out_dtype),
        compiler_params=_cparams("parallel", "parallel", "arbitrary"),
        name=name,
    )(a, w)


def _pool_kernel(prev_ref, cur_ref, next_ref, pw_ref, ps_ref, o_ref, ext_ref, *, tm, nt, seq_len):
    ti = pl.program_id(1)
    cur = cur_ref[...]
    ext_ref[SUBLANES:SUBLANES + tm, :] = cur
    ext_ref[0:SUBLANES, :] = jnp.where(ti > 0, prev_ref[...], 0.0)
    ext_ref[SUBLANES + tm:, :] = jnp.where(ti < nt - 1, next_ref[...], 0.0)
    t = ti * tm + lax.broadcasted_iota(jnp.int32, (tm, 1), 0)
    for g, w in enumerate(POOL_WINDOWS):
        r = w // 2
        c0 = g * POOL_GROUP_W
        s = ext_ref[SUBLANES - r:SUBLANES - r + tm, c0:c0 + POOL_GROUP_W]
        for dlt in range(-r + 1, r):
            s = s + ext_ref[SUBLANES + dlt:SUBLANES + dlt + tm, c0:c0 + POOL_GROUP_W]
        cnt = (jnp.minimum(t + r, seq_len) - jnp.maximum(t - r, 0)).astype(F32)
        pooled = s / cnt - cur[:, c0:c0 + POOL_GROUP_W]
        y = jnp.dot(pooled.astype(BF16), pw_ref[g], preferred_element_type=F32)
        o_ref[:, c0:c0 + POOL_GROUP_W] = (y * ps_ref[:, c0:c0 + POOL_GROUP_W]).astype(o_ref.dtype)


def _pool_mix(zs, pool_w, pool_scale, *, tm):
    b, l, _ = zs.shape
    tm = min(tm, l)
    nt = l // tm
    hb = tm // SUBLANES
    return pl.pallas_call(
        functools.partial(_pool_kernel, tm=tm, nt=nt, seq_len=l),
        grid=(b, nt),
        in_specs=[
            pl.BlockSpec((None, SUBLANES, POOL_W), lambda bi, i: (bi, jnp.maximum(i * hb - 1, 0), 0)),
            pl.BlockSpec((None, tm, POOL_W), lambda bi, i: (bi, i, 0)),
            pl.BlockSpec((None, SUBLANES, POOL_W), lambda bi, i: (bi, jnp.minimum((i + 1) * hb, l // SUBLANES - 1), 0)),
            pl.BlockSpec((len(POOL_WINDOWS), POOL_GROUP_W, POOL_GROUP_W), lambda bi, i: (0, 0, 0)),
            pl.BlockSpec((1, POOL_W), lambda bi, i: (0, 0)),
        ],
        out_specs=pl.BlockSpec((None, tm, POOL_W), lambda bi, i: (bi, i, 0)),
        out_shape=jax.ShapeDtypeStruct((b, l, POOL_W), BF16),
        scratch_shapes=[pltpu.VMEM((tm + 2 * SUBLANES, POOL_W), F32)],
        compiler_params=_cparams("parallel", "parallel"),
        name="pool_mix",
    )(zs, zs, zs, pool_w, pool_scale.reshape(1, POOL_W))


def _sg_kernel(z_ref, g_ref, w_ref, b_ref, o_ref, *, tm):
    z = z_ref[...]
    a = z * (0.5 * (1.0 + jnp.tanh(math.sqrt(2.0 / math.pi) * (z + 0.044715 * (z * z * z)))))
    u = a[:, :SG_W]
    v = a[:, SG_W:]
    vc = v - jnp.mean(v, axis=-1, keepdims=True)
    vn = (vc * lax.rsqrt(jnp.mean(vc * vc, axis=-1, keepdims=True) + EPS) * g_ref[...]).astype(BF16)
    for c in range(tm // CHUNK):
        rows = slice(c * CHUNK, (c + 1) * CHUNK)
        for g in range(SG_GROUPS):
            cols = slice(g * SG_GROUP_W, (g + 1) * SG_GROUP_W)
            s = jnp.dot(w_ref[g], vn[rows, cols], preferred_element_type=F32) + b_ref[g]
            o_ref[rows, cols] = (u[rows, cols] * s).astype(o_ref.dtype)


def _spatial_gating(zs, sg_norm_g, sg_w, sg_b, *, tm):
    b, l, _ = zs.shape
    tm = min(tm, l)
    return pl.pallas_call(
        functools.partial(_sg_kernel, tm=tm),
        grid=(b, l // tm),
        in_specs=[
            pl.BlockSpec((None, tm, 2 * SG_W), lambda bi, i: (bi, i, ZS_SG // (2 * SG_W))),
            pl.BlockSpec((1, SG_W), lambda bi, i: (0, 0)),
            pl.BlockSpec((SG_GROUPS, CHUNK, CHUNK), lambda bi, i: (0, 0, 0)),
            pl.BlockSpec((SG_GROUPS, CHUNK, 1), lambda bi, i: (0, 0, 0)),
        ],
        out_specs=pl.BlockSpec((None, tm, SG_W), lambda bi, i: (bi, i, 0)),
        out_shape=jax.ShapeDtypeStruct((b, l, SG_W), BF16),
        compiler_params=_cparams("parallel", "parallel"),
        name="spatial_gating",
    )(zs, sg_norm_g.reshape(1, SG_W), sg_w, sg_b.reshape(SG_GROUPS, CHUNK, 1))


def _q_kernel(z_ref, gl_ref, w_ref, gn_ref, gr_ref, grs_ref, cos_ref, sin_ref, o_ref, *, out_scale):
    z = z_ref[...]
    zn = (z * _rms_scale(z, Q_LORA) * gl_ref[...]).astype(BF16)
    q = jnp.dot(zn, w_ref[...], preferred_element_type=F32)
    nope_w = N_HEADS * QK_NOPE
    rope_w = N_HEADS * QK_ROPE
    lo = lax.broadcasted_iota(jnp.int32, (1, LANES), 1) < QK_ROPE
    cos = cos_ref[...]
    sin = sin_ref[...]
    gn = gn_ref[...] * out_scale
    for pair in range(N_HEADS // 2):
        blk = slice(pair * LANES, (pair + 1) * LANES)
        r2 = q[:, nope_w + pair * LANES:nope_w + (pair + 1) * LANES]
        r2s = q[:, nope_w + rope_w + pair * LANES:nope_w + rope_w + (pair + 1) * LANES]
        sq = r2 * r2
        ss_lo = jnp.sum(jnp.where(lo, sq, 0.0), axis=-1, keepdims=True)
        ss_hi = jnp.sum(jnp.where(lo, 0.0, sq), axis=-1, keepdims=True)
        rot = r2 * gr_ref[:, blk] * cos + r2s * grs_ref[:, blk] * sin
        for half, ss_r in enumerate((ss_lo, ss_hi)):
            h = 2 * pair + half
            nope = q[:, h * QK_NOPE:(h + 1) * QK_NOPE]
            inv = lax.rsqrt((jnp.sum(nope * nope, axis=-1, keepdims=True) + ss_r) * (1.0 / QK_HEAD) + EPS)
            o_ref[h, :, 0:LANES] = (nope * inv * gn).astype(o_ref.dtype)
            keep = lo if half == 0 else jnp.logical_not(lo)
            o_ref[h, :, LANES:HEAD_PAD] = jnp.where(keep, rot * (inv * out_scale), 0.0).astype(o_ref.dtype)


def _q_proj(zs, q_lat_g, w_q, gn, gr, grs, cos_tab, sin_tab, *, out_scale, tm):
    b, l, _ = zs.shape
    tm = min(tm, l)
    return pl.pallas_call(
        functools.partial(_q_kernel, out_scale=out_scale),
        grid=(b, l // tm),
        in_specs=[
            pl.BlockSpec((None, tm, Q_LORA), lambda bi, i: (bi, i, ZS_Q // Q_LORA)),
            pl.BlockSpec((1, Q_LORA), lambda bi, i: (0, 0)),
            pl.BlockSpec(w_q.shape, lambda bi, i: (0, 0)),
            pl.BlockSpec((1, QK_NOPE), lambda bi, i: (0, 0)),
            pl.BlockSpec((1, N_HEADS * QK_ROPE), lambda bi, i: (0, 0)),
            pl.BlockSpec((1, N_HEADS * QK_ROPE), lambda bi, i: (0, 0)),
            pl.BlockSpec((tm, LANES), lambda bi, i: (i, 0)),
            pl.BlockSpec((tm, LANES), lambda bi, i: (i, 0)),
        ],
        out_specs=pl.BlockSpec((None, N_HEADS, tm, HEAD_PAD), lambda bi, i: (bi, 0, i, 0)),
        out_shape=jax.ShapeDtypeStruct((b, N_HEADS, l, HEAD_PAD), BF16),
        compiler_params=_cparams("parallel", "parallel"),
        name="q_proj",
    )(zs, q_lat_g.reshape(1, Q_LORA), w_q, gn, gr, grs, cos_tab, sin_tab)


def _kv_kernel(z_ref, gl_ref, w_ref, gn_ref, gt_ref, tab_ref, k_ref, v_ref):
    z = z_ref[...]
    zkv = z[:, :KV_LORA]
    zn = (zkv * _rms_scale(zkv, KV_LORA) * gl_ref[...]).astype(BF16)
    kv = jnp.dot(zn, w_ref[...], preferred_element_type=F32)
    lo = lax.broadcasted_iota(jnp.int32, (1, LANES), 1) < QK_ROPE
    kr2 = z[:, KV_LORA:KV_LORA + LANES]
    ss_r = jnp.sum(jnp.where(lo, kr2 * kr2, 0.0), axis=-1, keepdims=True)
    t = kr2 * gt_ref[...] * tab_ref[...]
    rot = t + pltpu.roll(t, QK_ROPE, axis=1)
    gn = gn_ref[...]
    ones = jnp.ones((z.shape[0], LANES), v_ref.dtype)
    for h in range(N_HEADS):
        nope = kv[:, h * QK_NOPE:(h + 1) * QK_NOPE]
        inv = lax.rsqrt((jnp.sum(nope * nope, axis=-1, keepdims=True) + ss_r) * (1.0 / QK_HEAD) + EPS)
        k_ref[h, :, 0:LANES] = (nope * inv * gn).astype(k_ref.dtype)
        keep = lo if h % 2 == 0 else jnp.logical_not(lo)
        k_ref[h, :, LANES:HEAD_PAD] = jnp.where(keep, rot * inv, 0.0).astype(k_ref.dtype)
        v_ref[h, :, 0:LANES] = kv[:, N_HEADS * QK_NOPE + h * V_HEAD:N_HEADS * QK_NOPE + (h + 1) * V_HEAD].astype(
            v_ref.dtype)
        v_ref[h, :, LANES:HEAD_PAD] = ones


def _kv_proj(zs, kv_lat_g, w_kv, gn, gt, tab, *, tm):
    b, l, _ = zs.shape
    tm = min(tm, l)
    blk = KV_LORA + 2 * LANES
    shape = jax.ShapeDtypeStruct((b, N_HEADS, l, HEAD_PAD), BF16)
    spec = pl.BlockSpec((None, N_HEADS, tm, HEAD_PAD), lambda bi, i: (bi, 0, i, 0))
    return pl.pallas_call(
        _kv_kernel,
        grid=(b, l // tm),
        in_specs=[
            pl.BlockSpec((None, tm, blk), lambda bi, i: (bi, i, ZS_KV // blk)),
            pl.BlockSpec((1, KV_LORA), lambda bi, i: (0, 0)),
            pl.BlockSpec(w_kv.shape, lambda bi, i: (0, 0)),
            pl.BlockSpec((1, QK_NOPE), lambda bi, i: (0, 0)),
            pl.BlockSpec((1, LANES), lambda bi, i: (0, 0)),
            pl.BlockSpec((tm, LANES), lambda bi, i: (i, 0)),
        ],
        out_specs=[spec, spec],
        out_shape=[shape, shape],
        compiler_params=_cparams("parallel", "parallel"),
        name="kv_proj",
    )(zs, kv_lat_g.reshape(1, KV_LORA), w_kv, gn, gt, tab)


def _attn_kernel(q_ref, *refs, chunks):
    n_src = len(chunks)
    o_ref, acc_ref, m_ref = refs[2 * n_src:]
    q = q_ref[...]
    acc_ref[...] = jnp.zeros_like(acc_ref)
    m_ref[...] = jnp.full_like(m_ref, -jnp.inf)

    def step(k, v):
        s = lax.dot_general(q, k, (((1,), (1,)), ((), ())), preferred_element_type=F32)
        m_prev = m_ref[...]
        m_new = jnp.maximum(m_prev, jnp.max(s, axis=-1, keepdims=True))
        alpha = jnp.exp2(m_prev - m_new)
        p = jnp.exp2(s - jnp.concatenate([m_new] * (s.shape[1] // LANES), axis=1)).astype(BF16)
        pv = jnp.dot(p, v, preferred_element_type=F32)
        acc_ref[...] = acc_ref[...] * jnp.concatenate([alpha, alpha], axis=1) + pv
        m_ref[...] = m_new

    for si, (bk, n) in enumerate(chunks):
        k_ref, v_ref = refs[2 * si], refs[2 * si + 1]
        if n == 1:
            step(k_ref[...], v_ref[...])
        else:
            def body(c, carry, k_ref=k_ref, v_ref=v_ref, bk=bk):
                off = pl.multiple_of(c * bk, bk)
                step(k_ref[pl.ds(off, bk), :], v_ref[pl.ds(off, bk), :])
                return carry

            lax.fori_loop(0, n, body, 0, unroll=16)
    acc = acc_ref[...]
    o_ref[...] = (acc[:, :V_HEAD] / acc[:, V_HEAD:]).astype(o_ref.dtype)


def _attention(q, sources, *, bq, bk):
    b, h, l, _ = q.shape
    bq = min(bq, l)
    chunks = []
    in_specs = [pl.BlockSpec((None, None, bq, HEAD_PAD), lambda bi, hi, i: (bi, hi, i, 0))]
    args = [q]
    for k, v in sources:
        lk = k.shape[2]
        cb = min(bk, lk)
        chunks.append((cb, lk // cb))
        for a in (k, v):
            in_specs.append(pl.BlockSpec((None, None, lk, HEAD_PAD), lambda bi, hi, i: (bi, hi, 0, 0)))
            args.append(a)
    return pl.pallas_call(
        functools.partial(_attn_kernel, chunks=tuple(chunks)),
        grid=(b, h, l // bq),
        in_specs=in_specs,
        out_specs=pl.BlockSpec((None, bq, V_HEAD), lambda bi, hi, i: (bi, i, hi)),
        out_shape=jax.ShapeDtypeStruct((b, l, h * V_HEAD), BF16),
        scratch_shapes=[pltpu.VMEM((bq, HEAD_PAD), F32), pltpu.VMEM((bq, LANES), F32)],
        compiler_params=_cparams("parallel", "parallel", "arbitrary"),
        name="latent_attention",
    )(*args)


def _merge_out_kernel(p_ref, a_ref, s_ref, gp_ref, ga_ref, gs_ref, wp_ref, wa_ref, ws_ref, wo_ref, x_ref, g1_ref,
                      o_ref, y_ref, *, nj, tn):
    j = pl.program_id(2)

    @pl.when(j < nj)
    def _():
        y = gp_ref[...].astype(F32) * jnp.dot(p_ref[...], wp_ref[...], preferred_element_type=F32)
        y += ga_ref[...].astype(F32) * jnp.dot(a_ref[...], wa_ref[...], preferred_element_type=F32)
        y += gs_ref[...].astype(F32) * jnp.dot(s_ref[...], ws_ref[...], preferred_element_type=F32)
        y_ref[j] = y.astype(y_ref.dtype)

    @pl.when(j >= nj)
    def _():
        acc = jnp.dot(y_ref[0], wo_ref[0:tn, :], preferred_element_type=F32)
        for c in range(1, nj):
            acc += jnp.dot(y_ref[c], wo_ref[c * tn:(c + 1) * tn, :], preferred_element_type=F32)
        o_ref[...] = x_ref[...] + g1_ref[...] * acc


def _merge_out(pool_out, attn, sg_out, gates, w_p, w_a, w_s, w_o, layer, x, mod, *, tm, tn):
    b, l, _ = pool_out.shape
    d = w_p.shape[2]
    tm = min(tm, l)
    nj = d // tn

    def first(j):
        return jnp.minimum(j, nj - 1)

    def second(j):
        return jnp.maximum(j - nj, 0)

    def act(width):
        return pl.BlockSpec((None, tm, width), lambda bi, i, j: (bi, i, 0))

    def gate(branch):
        return pl.BlockSpec((None, tm, tn), lambda bi, i, j: (bi, i, branch * nj + first(j)))

    def wgt(width):
        return pl.BlockSpec((None, width, tn), lambda bi, i, j: (layer, 0, first(j)))

    return pl.pallas_call(
        functools.partial(_merge_out_kernel, nj=nj, tn=tn),
        grid=(b, l // tm, 2 * nj),
        in_specs=[act(POOL_W), act(N_HEADS * V_HEAD), act(SG_W), gate(0), gate(1), gate(2),
                  wgt(POOL_W), wgt(N_HEADS * V_HEAD), wgt(SG_W),
                  pl.BlockSpec((None, d, tn), lambda bi, i, j: (layer, 0, second(j))),
                  pl.BlockSpec((None, tm, tn), lambda bi, i, j: (bi, i, second(j))),
                  pl.BlockSpec((None, 1, tn), lambda bi, i, j: (bi, 0, second(j)))],
        out_specs=pl.BlockSpec((None, tm, tn), lambda bi, i, j: (bi, i, second(j))),
        out_shape=jax.ShapeDtypeStruct((b, l, d), F32),
        scratch_shapes=[pltpu.VMEM((nj, tm, tn), BF16)],
        compiler_params=_cparams("parallel", "parallel", "arbitrary"),
        name="merge_out_proj",
    )(pool_out, attn, sg_out, gates, gates, gates, w_p, w_a, w_s, w_o, x, mod[:, 2:3, :])


def _mm_res_kernel(a_ref, w_ref, x_ref, gate_ref, o_ref):
    y = jnp.dot(a_ref[...], w_ref[...], preferred_element_type=F32)
    o_ref[...] = x_ref[...] + gate_ref[...] * y


def _mm_residual(a, w, layer, x, mod, gate_row, *, tm, tn, name):
    b, l, k = a.shape
    d = w.shape[2]
    tm = min(tm, l)
    return pl.pallas_call(
        _mm_res_kernel,
        grid=(b, l // tm, d // tn),
        in_specs=[
            pl.BlockSpec((None, tm, k), lambda bi, i, j: (bi, i, 0)),
            pl.BlockSpec((None, k, tn), lambda bi, i, j: (layer, 0, j)),
            pl.BlockSpec((None, tm, tn), lambda bi, i, j: (bi, i, j)),
            pl.BlockSpec((None, 1, tn), lambda bi, i, j: (bi, 0, j)),
        ],
        out_specs=pl.BlockSpec((None, tm, tn), lambda bi, i, j: (bi, i, j)),
        out_shape=jax.ShapeDtypeStruct((b, l, d), F32),
        compiler_params=_cparams("parallel", "parallel", "arbitrary"),
        name=name,
    )(a, w, x, mod[:, gate_row:gate_row + 1, :])


def _ffn_up_kernel(h_ref, wa_ref, wv_ref, cwa_ref, cwv_ref, cba_ref, cbv_ref, o_ref, *, tm):
    halo = BF16_ROWS
    h = h_ref[...]
    rows = h.shape[0]

    def conv(w_ref, cw_ref, cb_ref):
        u = jnp.dot(h, w_ref[...], preferred_element_type=F32)
        acc = cb_ref[...] + pltpu.roll(u, 1, axis=0) * cw_ref[0:1, :]
        acc = acc + u * cw_ref[1:2, :]
        acc = acc + pltpu.roll(u, rows - 1, axis=0) * cw_ref[2:3, :]
        return acc[halo:halo + tm, :]

    a = conv(wa_ref, cwa_ref, cba_ref)
    v = conv(wv_ref, cwv_ref, cbv_ref)
    o_ref[...] = (a * _sigmoid(a) * v).astype(o_ref.dtype)


def _ffn_up(h_tiles, w_up, layer, conv_w, conv_b, *, tf):
    b, nt, rows, d = h_tiles.shape
    tm = rows - 2 * BF16_ROWS
    l = nt * tm
    d_ff = w_up.shape[2] // 2
    nf = d_ff // tf
    return pl.pallas_call(
        functools.partial(_ffn_up_kernel, tm=tm),
        grid=(b, nt, nf),
        in_specs=[
            pl.BlockSpec((None, None, rows, d), lambda bi, i, j: (bi, i, 0, 0)),
            pl.BlockSpec((None, d, tf), lambda bi, i, j: (layer, 0, j)),
            pl.BlockSpec((None, d, tf), lambda bi, i, j: (layer, 0, nf + j)),
            pl.BlockSpec((CONV_W, tf), lambda bi, i, j: (0, j)),
            pl.BlockSpec((CONV_W, tf), lambda bi, i, j: (0, nf + j)),
            pl.BlockSpec((1, tf), lambda bi, i, j: (0, j)),
            pl.BlockSpec((1, tf), lambda bi, i, j: (0, nf + j)),
        ],
        out_specs=pl.BlockSpec((None, tm, tf), lambda bi, i, j: (bi, i, j)),
        out_shape=jax.ShapeDtypeStruct((b, l, d_ff), BF16),
        compiler_params=_cparams("parallel", "parallel", "arbitrary"),
        name="ffn_up_conv_gate",
    )(h_tiles, w_up, w_up, conv_w, conv_w, conv_b.reshape(1, -1), conv_b.reshape(1, -1))


def _rope_tables(n_tokens, rotate):
    if rotate:
        t = np.arange(n_tokens)
        inv_freq = ROPE_THETA ** (-np.arange(ROPE_FREQS, dtype=np.float64) / ROPE_FREQS)
        ar = (t // GRID_W)[:, None] * inv_freq
        ac = (t % GRID_W)[:, None] * inv_freq
        c = np.concatenate([np.cos(ar), np.cos(ar), np.cos(ac), np.cos(ac)], axis=1)
        s = np.concatenate([-np.sin(ar), np.sin(ar), -np.sin(ac), np.sin(ac)], axis=1)
    else:
        c, s = np.ones((n_tokens, QK_ROPE)), np.zeros((n_tokens, QK_ROPE))
    cat = lambda a, b_: jnp.asarray(np.concatenate([a, b_], axis=1), F32)
    return dict(q_cos=cat(c, c), q_sin=cat(s, s), k=cat(c, s))


def _stacked_weights(w_in, w_br_pool, w_br_mla, w_br_sg, w_o, ffn_up, ffn_down):
    depth, d, _ = w_in.shape
    o_q, o_kv, o_kr, o_sg, o_gate = POOL_W, POOL_W + Q_LORA, POOL_W + Q_LORA + KV_LORA, \
        POOL_W + Q_LORA + KV_LORA + QK_ROPE, POOL_W + Q_LORA + KV_LORA + QK_ROPE + 2 * SG_W
    w_kr = w_in[:, :, o_kr:o_sg]
    w_small = jnp.concatenate([
        w_in[:, :, :o_q], w_in[:, :, o_q:o_kv], w_in[:, :, o_sg:o_gate], w_in[:, :, o_kv:o_kr], w_kr,
        w_kr[:, :, ROPE_SWAP], jnp.zeros((depth, d, ZS_COLS - ZS_KV - KV_LORA - 2 * QK_ROPE), w_in.dtype)],
        axis=2).astype(BF16)
    return dict(
        w_small=w_small, w_gate=w_in[:, :, o_gate:].astype(BF16), w_br_pool=w_br_pool.astype(BF16),
        w_br_mla=w_br_mla.astype(BF16), w_br_sg=w_br_sg.astype(BF16), w_o=w_o.astype(BF16),
        ffn_up=ffn_up.astype(BF16), ffn_down=ffn_down.astype(BF16))


def _layer_weights(p):
    w_uq = p["w_uq"].reshape(Q_LORA, N_HEADS, QK_HEAD)
    w_q_rope = w_uq[:, :, QK_NOPE:]
    w_q = jnp.concatenate([
        w_uq[:, :, :QK_NOPE].reshape(Q_LORA, -1), w_q_rope.reshape(Q_LORA, -1),
        w_q_rope[:, :, ROPE_SWAP].reshape(Q_LORA, -1)], axis=1).astype(BF16)
    gq = p["q_norm_g"]
    gq_n = gq[:QK_NOPE].reshape(1, QK_NOPE)
    gq_r = jnp.tile(gq[QK_NOPE:], N_HEADS).reshape(1, -1)
    gq_rs = jnp.tile(gq[QK_NOPE:][ROPE_SWAP], N_HEADS).reshape(1, -1)

    w_ukv = p["w_ukv"].reshape(KV_LORA, N_HEADS, QK_NOPE + V_HEAD)
    w_kv = jnp.concatenate([
        w_ukv[:, :, :QK_NOPE].reshape(KV_LORA, -1), w_ukv[:, :, QK_NOPE:].reshape(KV_LORA, -1)], axis=1).astype(BF16)
    gk = p["k_norm_g"]
    gk_n = gk[:QK_NOPE].reshape(1, QK_NOPE)
    gk_t = jnp.concatenate([gk[QK_NOPE:], gk[QK_NOPE:][ROPE_SWAP]]).reshape(1, LANES)
    return dict(
        w_q=w_q, gq_n=gq_n, gq_r=gq_r, gq_rs=gq_rs, w_kv=w_kv, gk_n=gk_n, gk_t=gk_t,
        pool_w=p["pool_w"].astype(BF16), sg_w=p["sg_w"].astype(BF16))


def _keys_values(zs, p, w, tabs):
    return _kv_proj(zs, p["kv_lat_g"], w["w_kv"], w["gk_n"], w["gk_t"], tabs["k"], tm=512)


def _queries(zs, p, w, tabs):
    out_scale = QK_HEAD ** -0.5 * math.log2(math.e)
    return _q_proj(zs, p["q_lat_g"], w["w_q"], w["gq_n"], w["gq_r"], w["gq_rs"], tabs["q_cos"], tabs["q_sin"],
                   out_scale=out_scale, tm=512)


def _in_proj(x, p, ws, layer, mod, *, gates):
    h = _norm_modulate(x, p["norm1_g"], mod, shift_row=0, scale_row=1, tm=1024)
    zs = _matmul(h, ws["w_small"], layer, sigmoid=False, out_dtype=F32, tm=1024, tn=512, name="in_proj_small")
    g = None
    if gates:
        g = _matmul(h, ws["w_gate"], layer, sigmoid=True, out_dtype=BF16, tm=1024, tn=512, name="in_proj_gate")
    return zs, g


def _mix_and_ffn(x, zs, gates, attn, p, w, ws, layer, mod):
    pool_out = _pool_mix(zs, w["pool_w"], p["pool_scale"], tm=512)
    sg_out = _spatial_gating(zs, p["sg_norm_g"], w["sg_w"], p["sg_b"], tm=512)
    x = _merge_out(pool_out, attn, sg_out, gates, ws["w_br_pool"], ws["w_br_mla"], ws["w_br_sg"], ws["w_o"], layer,
                   x, mod, tm=1024, tn=512)
    h2 = _norm_modulate_tiles(x, p["norm2_g"], mod, shift_row=3, scale_row=4, tm=1024)
    act = _ffn_up(h2, ws["ffn_up"], layer, p["ffn_conv_w"], p["ffn_conv_b"], tf=512)
    return _mm_residual(act, ws["ffn_down"], layer, x, mod, 5, tm=1024, tn=512, name="ffn_down_residual")


def kernel(x, c, ctx, c_ctx, ada_w, ada_b, norm1_g, w_in, pool_w, pool_scale, q_lat_g, w_uq, kv_lat_g, w_ukv, q_norm_g, k_norm_g, sg_norm_g, sg_w, sg_b, w_br_pool, w_br_mla, w_br_sg, w_o, norm2_g, ffn_up, ffn_conv_w, ffn_conv_b, ffn_down):
    b, l, d = x.shape
    lc = ctx.shape[1]
    depth = ada_w.shape[0]
    assert b + 1 <= SUBLANES

    cvec = jnp.concatenate([c, c_ctx[None, :], jnp.zeros((SUBLANES - b - 1, d), c.dtype)], axis=0)
    mod = _ada(cvec, ada_w, ada_b).reshape(depth, SUBLANES, 6, d)
    rope_lat = _rope_tables(l, rotate=True)
    rope_ctx = _rope_tables(lc, rotate=False)
    ws = _stacked_weights(w_in, w_br_pool, w_br_mla, w_br_sg, w_o, ffn_up, ffn_down)

    xl, xc = x, ctx
    for i in range(depth):
        p = {
            "norm1_g": norm1_g[i], "pool_w": pool_w[i], "pool_scale": pool_scale[i],
            "q_lat_g": q_lat_g[i], "w_uq": w_uq[i], "kv_lat_g": kv_lat_g[i], "w_ukv": w_ukv[i],
            "q_norm_g": q_norm_g[i], "k_norm_g": k_norm_g[i], "sg_norm_g": sg_norm_g[i], "sg_w": sg_w[i],
            "sg_b": sg_b[i], "norm2_g": norm2_g[i], "ffn_conv_w": ffn_conv_w[i], "ffn_conv_b": ffn_conv_b[i],
        }
        w = _layer_weights(p)
        mod_lat = mod[i, :b]
        mod_ctx = jnp.broadcast_to(mod[i, b:b + 1], (b, 6, d))

        zl, gl = _in_proj(xl, p, ws, i, mod_lat, gates=True)
        zc, gc = _in_proj(xc, p, ws, i, mod_ctx, gates=i < depth - 1)
        kc, vc = _keys_values(zc, p, w, rope_ctx)
        kl, vl = _keys_values(zl, p, w, rope_lat)
        ql = _queries(zl, p, w, rope_lat)
        attn_l = _attention(ql, [(kl, vl), (kc, vc)], bq=2048, bk=512)
        xl = _mix_and_ffn(xl, zl, gl, attn_l, p, w, ws, i, mod_lat)
        if i < depth - 1:
            qc = _queries(zc, p, w, rope_ctx)
            attn_c = _attention(qc, [(kc, vc)], bq=512, bk=512)
            xc = _mix_and_ffn(xc, zc, gc, attn_c, p, w, ws, i, mod_ctx)
    return xl
```

```python
import functools
import math

import jax
import jax.numpy as jnp
import numpy as np
from jax import lax
from jax.experimental import pallas as pl
from jax.experimental.pallas import tpu as pltpu

F32 = jnp.float32
BF16 = jnp.bfloat16

GRID_W = 64
EPS = 1e-6
POOL_W = 512
POOL_WINDOWS = (2, 4, 8, 16)
POOL_GROUP_W = POOL_W // len(POOL_WINDOWS)
N_HEADS = 8
Q_LORA = 512
KV_LORA = 256
QK_NOPE = 128
QK_ROPE = 64
QK_HEAD = QK_NOPE + QK_ROPE
V_HEAD = 128
ROPE_FREQS = QK_ROPE // 4
ROPE_THETA = 10000.0
SG_W = 512
SG_GROUPS = 4
SG_GROUP_W = SG_W // SG_GROUPS
CHUNK = 128
N_BRANCH = 3
CONV_W = 3

LANES = 128
SUBLANES = 8
BF16_ROWS = 16
HEAD_PAD = 2 * LANES
MERGE_TN = 512
VMEM_LIMIT_BYTES = 56 * 1024 * 1024

ZS_POOL = 0
ZS_Q = 512
ZS_SG = 1024
ZS_KV = 2048
ZS_COLS = 2560
ROPE_SWAP = np.concatenate([np.arange(16, 32), np.arange(0, 16), np.arange(48, 64), np.arange(32, 48)])


def _cparams(*sem):
    return pltpu.CompilerParams(dimension_semantics=sem, vmem_limit_bytes=VMEM_LIMIT_BYTES)


def _rms_scale(x, width):
    return lax.rsqrt(jnp.sum(x * x, axis=-1, keepdims=True) * (1.0 / width) + EPS)


def _ada_kernel(c_ref, w_ref, b_ref, o_ref):
    c = c_ref[...]
    s = c * jax.nn.sigmoid(c)
    o_ref[...] = jnp.dot(s, w_ref[...], preferred_element_type=F32, precision=lax.Precision.HIGHEST) + b_ref[...]


def _ada(cvec, ada_w, ada_b):
    depth, d, n = ada_w.shape
    tn = 1024
    return pl.pallas_call(
        _ada_kernel,
        grid=(depth, n // tn),
        in_specs=[
            pl.BlockSpec((SUBLANES, d), lambda l, j: (0, 0)),
            pl.BlockSpec((None, d, tn), lambda l, j: (l, 0, j)),
            pl.BlockSpec((None, 1, tn), lambda l, j: (l, 0, j)),
        ],
        out_specs=pl.BlockSpec((None, SUBLANES, tn), lambda l, j: (l, 0, j)),
        out_shape=jax.ShapeDtypeStruct((depth, SUBLANES, n), F32),
        compiler_params=_cparams("parallel", "arbitrary"),
        name="ada_modulation",
    )(cvec, ada_w, ada_b.reshape(depth, 1, n))


def _norm_mod(x, g, mod_ref, shift_row, scale_row):
    y = x * _rms_scale(x, x.shape[-1]) * g
    return y * (1.0 + mod_ref[scale_row:scale_row + 1, :]) + mod_ref[shift_row:shift_row + 1, :]


def _sigmoid(x):
    return 0.5 * jnp.tanh(0.5 * x) + 0.5


def _norm_kernel(x_ref, g_ref, mod_ref, o_ref, *, shift_row, scale_row):
    o_ref[...] = _norm_mod(x_ref[...], g_ref[...], mod_ref, shift_row, scale_row).astype(o_ref.dtype)


def _norm_modulate(x, g, mod, *, shift_row, scale_row, tm):
    b, l, d = x.shape
    tm = min(tm, l)
    return pl.pallas_call(
        functools.partial(_norm_kernel, shift_row=shift_row, scale_row=scale_row),
        grid=(b, l // tm),
        in_specs=[
            pl.BlockSpec((None, tm, d), lambda bi, i: (bi, i, 0)),
            pl.BlockSpec((1, d), lambda bi, i: (0, 0)),
            pl.BlockSpec((None, 6, d), lambda bi, i: (bi, 0, 0)),
        ],
        out_specs=pl.BlockSpec((None, tm, d), lambda bi, i: (bi, i, 0)),
        out_shape=jax.ShapeDtypeStruct((b, l, d), BF16),
        compiler_params=_cparams("parallel", "parallel"),
        name="norm_modulate",
    )(x, g.reshape(1, d), mod)


def _norm_halo_kernel(xp_ref, xc_ref, xn_ref, g_ref, mod_ref, o_ref, *, tm, nt, shift_row, scale_row):
    ti = pl.program_id(1)
    halo = BF16_ROWS
    g = g_ref[...]
    o_ref[halo:halo + tm, :] = _norm_mod(xc_ref[...], g, mod_ref, shift_row, scale_row).astype(o_ref.dtype)
    prev = jnp.where(ti > 0, _norm_mod(xp_ref[...], g, mod_ref, shift_row, scale_row), 0.0)
    o_ref[0:halo, :] = prev.astype(o_ref.dtype)
    nxt = jnp.where(ti < nt - 1, _norm_mod(xn_ref[...], g, mod_ref, shift_row, scale_row), 0.0)
    o_ref[halo + tm:, :] = nxt.astype(o_ref.dtype)


def _norm_modulate_tiles(x, g, mod, *, shift_row, scale_row, tm):
    b, l, d = x.shape
    tm = min(tm, l)
    nt = l // tm
    hb = tm // BF16_ROWS
    rows = tm + 2 * BF16_ROWS
    return pl.pallas_call(
        functools.partial(_norm_halo_kernel, tm=tm, nt=nt, shift_row=shift_row, scale_row=scale_row),
        grid=(b, nt),
        in_specs=[
            pl.BlockSpec((None, BF16_ROWS, d), lambda bi, i: (bi, jnp.maximum(i * hb - 1, 0), 0)),
            pl.BlockSpec((None, tm, d), lambda bi, i: (bi, i, 0)),
            pl.BlockSpec((None, BF16_ROWS, d), lambda bi, i: (bi, jnp.minimum((i + 1) * hb, l // BF16_ROWS - 1), 0)),
            pl.BlockSpec((1, d), lambda bi, i: (0, 0)),
            pl.BlockSpec((None, 6, d), lambda bi, i: (bi, 0, 0)),
        ],
        out_specs=pl.BlockSpec((None, None, rows, d), lambda bi, i: (bi, i, 0, 0)),
        out_shape=jax.ShapeDtypeStruct((b, nt, rows, d), BF16),
        compiler_params=_cparams("parallel", "parallel"),
        name="norm_modulate_tiles",
    )(x, x, x, g.reshape(1, d), mod)


def _matmul_kernel(a_ref, w_ref, o_ref, *, sigmoid):
    z = jnp.dot(a_ref[...], w_ref[...], preferred_element_type=F32)
    if sigmoid:
        z = _sigmoid(z)
    o_ref[...] = z.astype(o_ref.dtype)


def _matmul(a, w, layer, *, sigmoid, out_dtype, tm, tn, name):
    b, l, k = a.shape
    n = w.shape[2]
    tm = min(tm, l)
    return pl.pallas_call(
        functools.partial(_matmul_kernel, sigmoid=sigmoid),
        grid=(b, l // tm, n // tn),
        in_specs=[
            pl.BlockSpec((None, tm, k), lambda bi, i, j: (bi, i, 0)),
            pl.BlockSpec((None, k, tn), lambda bi, i, j: (layer, 0, j)),
        ],
        out_specs=pl.BlockSpec((None, tm, tn), lambda bi, i, j: (bi, i, j)),
        out_shape=jax.ShapeDtypeStruct((b, l, n), out_dtype),
        compiler_params=_cparams("parallel", "parallel", "arbitrary"),
        name=name,
    )(a, w)


def _pool_kernel(prev_ref, cur_ref, next_ref, pw_ref, ps_ref, o_ref, ext_ref, *, tm, nt, seq_len):
    ti = pl.program_id(1)
    cur = cur_ref[...]
    ext_ref[SUBLANES:SUBLANES + tm, :] = cur
    ext_ref[0:SUBLANES, :] = jnp.where(ti > 0, prev_ref[...], 0.0)
    ext_ref[SUBLANES + tm:, :] = jnp.where(ti < nt - 1, next_ref[...], 0.0)
    t = ti * tm + lax.broadcasted_iota(jnp.int32, (tm, 1), 0)
    for g, w in enumerate(POOL_WINDOWS):
        r = w // 2
        c0 = g * POOL_GROUP_W
        s = ext_ref[SUBLANES - r:SUBLANES - r + tm, c0:c0 + POOL_GROUP_W]
        for dlt in range(-r + 1, r):
            s = s + ext_ref[SUBLANES + dlt:SUBLANES + dlt + tm, c0:c0 + POOL_GROUP_W]
        cnt = (jnp.minimum(t + r, seq_len) - jnp.maximum(t - r, 0)).astype(F32)
        pooled = s / cnt - cur[:, c0:c0 + POOL_GROUP_W]
        y = jnp.dot(pooled.astype(BF16), pw_ref[g], preferred_element_type=F32)
        o_ref[:, c0:c0 + POOL_GROUP_W] = (y * ps_ref[:, c0:c0 + POOL_GROUP_W]).astype(o_ref.dtype)


def _pool_mix(zs, pool_w, pool_scale, *, tm):
    b, l, _ = zs.shape
    tm = min(tm, l)
    nt = l // tm
    hb = tm // SUBLANES
    return pl.pallas_call(
        functools.partial(_pool_kernel, tm=tm, nt=nt, seq_len=l),
        grid=(b, nt),
        in_specs=[
            pl.BlockSpec((None, SUBLANES, POOL_W), lambda bi, i: (bi, jnp.maximum(i * hb - 1, 0), 0)),
            pl.BlockSpec((None, tm, POOL_W), lambda bi, i: (bi, i, 0)),
            pl.BlockSpec((None, SUBLANES, POOL_W), lambda bi, i: (bi, jnp.minimum((i + 1) * hb, l // SUBLANES - 1), 0)),
            pl.BlockSpec((len(POOL_WINDOWS), POOL_GROUP_W, POOL_GROUP_W), lambda bi, i: (0, 0, 0)),
            pl.BlockSpec((1, POOL_W), lambda bi, i: (0, 0)),
        ],
        out_specs=pl.BlockSpec((None, tm, POOL_W), lambda bi, i: (bi, i, 0)),
        out_shape=jax.ShapeDtypeStruct((b, l, POOL_W), BF16),
        scratch_shapes=[pltpu.VMEM((tm + 2 * SUBLANES, POOL_W), F32)],
        compiler_params=_cparams("parallel", "parallel"),
        name="pool_mix",
    )(zs, zs, zs, pool_w, pool_scale.reshape(1, POOL_W))


def _sg_kernel(z_ref, g_ref, w_ref, b_ref, o_ref, *, tm):
    z = z_ref[...]
    a = z * (0.5 * (1.0 + jnp.tanh(math.sqrt(2.0 / math.pi) * (z + 0.044715 * (z * z * z)))))
    u = a[:, :SG_W]
    v = a[:, SG_W:]
    vc = v - jnp.mean(v, axis=-1, keepdims=True)
    vn = (vc * lax.rsqrt(jnp.mean(vc * vc, axis=-1, keepdims=True) + EPS) * g_ref[...]).astype(BF16)
    for c in range(tm // CHUNK):
        rows = slice(c * CHUNK, (c + 1) * CHUNK)
        for g in range(SG_GROUPS):
            cols = slice(g * SG_GROUP_W, (g + 1) * SG_GROUP_W)
            s = jnp.dot(w_ref[g], vn[rows, cols], preferred_element_type=F32) + b_ref[g]
            o_ref[rows, cols] = (u[rows, cols] * s).astype(o_ref.dtype)


def _spatial_gating(zs, sg_norm_g, sg_w, sg_b, *, tm):
    b, l, _ = zs.shape
    tm = min(tm, l)
    return pl.pallas_call(
        functools.partial(_sg_kernel, tm=tm),
        grid=(b, l // tm),
        in_specs=[
            pl.BlockSpec((None, tm, 2 * SG_W), lambda bi, i: (bi, i, ZS_SG // (2 * SG_W))),
            pl.BlockSpec((1, SG_W), lambda bi, i: (0, 0)),
            pl.BlockSpec((SG_GROUPS, CHUNK, CHUNK), lambda bi, i: (0, 0, 0)),
            pl.BlockSpec((SG_GROUPS, CHUNK, 1), lambda bi, i: (0, 0, 0)),
        ],
        out_specs=pl.BlockSpec((None, tm, SG_W), lambda bi, i: (bi, i, 0)),
        out_shape=jax.ShapeDtypeStruct((b, l, SG_W), BF16),
        compiler_params=_cparams("parallel", "parallel"),
        name="spatial_gating",
    )(zs, sg_norm_g.reshape(1, SG_W), sg_w, sg_b.reshape(SG_GROUPS, CHUNK, 1))


def _q_kernel(z_ref, gl_ref, w_ref, gn_ref, gr_ref, grs_ref, cos_ref, sin_ref, o_ref, *, out_scale):
    z = z_ref[...]
    zn = (z * _rms_scale(z, Q_LORA) * gl_ref[...]).astype(BF16)
    q = jnp.dot(zn, w_ref[...], preferred_element_type=F32)
    nope_w = N_HEADS * QK_NOPE
    rope_w = N_HEADS * QK_ROPE
    lo = lax.broadcasted_iota(jnp.int32, (1, LANES), 1) < QK_ROPE
    cos = cos_ref[...]
    sin = sin_ref[...]
    gn = gn_ref[...] * out_scale
    for pair in range(N_HEADS // 2):
        blk = slice(pair * LANES, (pair + 1) * LANES)
        r2 = q[:, nope_w + pair * LANES:nope_w + (pair + 1) * LANES]
        r2s = q[:, nope_w + rope_w + pair * LANES:nope_w + rope_w + (pair + 1) * LANES]
        sq = r2 * r2
        ss_lo = jnp.sum(jnp.where(lo, sq, 0.0), axis=-1, keepdims=True)
        ss_hi = jnp.sum(jnp.where(lo, 0.0, sq), axis=-1, keepdims=True)
        rot = r2 * gr_ref[:, blk] * cos + r2s * grs_ref[:, blk] * sin
        for half, ss_r in enumerate((ss_lo, ss_hi)):
            h = 2 * pair + half
            nope = q[:, h * QK_NOPE:(h + 1) * QK_NOPE]
            inv = lax.rsqrt((jnp.sum(nope * nope, axis=-1, keepdims=True) + ss_r) * (1.0 / QK_HEAD) + EPS)
            o_ref[h, :, 0:LANES] = (nope * inv * gn).astype(o_ref.dtype)
            keep = lo if half == 0 else jnp.logical_not(lo)
            o_ref[h, :, LANES:HEAD_PAD] = jnp.where(keep, rot * (inv * out_scale), 0.0).astype(o_ref.dtype)


def _q_proj(zs, q_lat_g, w_q, gn, gr, grs, cos_tab, sin_tab, *, out_scale, tm):
    b, l, _ = zs.shape
    tm = min(tm, l)
    return pl.pallas_call(
        functools.partial(_q_kernel, out_scale=out_scale),
        grid=(b, l // tm),
        in_specs=[
            pl.BlockSpec((None, tm, Q_LORA), lambda bi, i: (bi, i, ZS_Q // Q_LORA)),
            pl.BlockSpec((1, Q_LORA), lambda bi, i: (0, 0)),
            pl.BlockSpec(w_q.shape, lambda bi, i: (0, 0)),
            pl.BlockSpec((1, QK_NOPE), lambda bi, i: (0, 0)),
            pl.BlockSpec((1, N_HEADS * QK_ROPE), lambda bi, i: (0, 0)),
            pl.BlockSpec((1, N_HEADS * QK_ROPE), lambda bi, i: (0, 0)),
            pl.BlockSpec((tm, LANES), lambda bi, i: (i, 0)),
            pl.BlockSpec((tm, LANES), lambda bi, i: (i, 0)),
        ],
        out_specs=pl.BlockSpec((None, N_HEADS, tm, HEAD_PAD), lambda bi, i: (bi, 0, i, 0)),
        out_shape=jax.ShapeDtypeStruct((b, N_HEADS, l, HEAD_PAD), BF16),
        compiler_params=_cparams("parallel", "parallel"),
        name="q_proj",
    )(zs, q_lat_g.reshape(1, Q_LORA), w_q, gn, gr, grs, cos_tab, sin_tab)


def _kv_kernel(z_ref, gl_ref, w_ref, gn_ref, gt_ref, tab_ref, k_ref, v_ref):
    z = z_ref[...]
    zkv = z[:, :KV_LORA]
    zn = (zkv * _rms_scale(zkv, KV_LORA) * gl_ref[...]).astype(BF16)
    kv = jnp.dot(zn, w_ref[...], preferred_element_type=F32)
    lo = lax.broadcasted_iota(jnp.int32, (1, LANES), 1) < QK_ROPE
    kr2 = z[:, KV_LORA:KV_LORA + LANES]
    ss_r = jnp.sum(jnp.where(lo, kr2 * kr2, 0.0), axis=-1, keepdims=True)
    t = kr2 * gt_ref[...] * tab_ref[...]
    rot = t + pltpu.roll(t, QK_ROPE, axis=1)
    gn = gn_ref[...]
    ones = jnp.ones((z.shape[0], LANES), v_ref.dtype)
    for h in range(N_HEADS):
        nope = kv[:, h * QK_NOPE:(h + 1) * QK_NOPE]
        inv = lax.rsqrt((jnp.sum(nope * nope, axis=-1, keepdims=True) + ss_r) * (1.0 / QK_HEAD) + EPS)
        k_ref[h, :, 0:LANES] = (nope * inv * gn).astype(k_ref.dtype)
        keep = lo if h % 2 == 0 else jnp.logical_not(lo)
        k_ref[h, :, LANES:HEAD_PAD] = jnp.where(keep, rot * inv, 0.0).astype(k_ref.dtype)
        v_ref[h, :, 0:LANES] = kv[:, N_HEADS * QK_NOPE + h * V_HEAD:N_HEADS * QK_NOPE + (h + 1) * V_HEAD].astype(
            v_ref.dtype)
        v_ref[h, :, LANES:HEAD_PAD] = ones


def _kv_proj(zs, kv_lat_g, w_kv, gn, gt, tab, *, tm):
    b, l, _ = zs.shape
    tm = min(tm, l)
    blk = KV_LORA + 2 * LANES
    shape = jax.ShapeDtypeStruct((b, N_HEADS, l, HEAD_PAD), BF16)
    spec = pl.BlockSpec((None, N_HEADS, tm, HEAD_PAD), lambda bi, i: (bi, 0, i, 0))
    return pl.pallas_call(
        _kv_kernel,
        grid=(b, l // tm),
        in_specs=[
            pl.BlockSpec((None, tm, blk), lambda bi, i: (bi, i, ZS_KV // blk)),
            pl.BlockSpec((1, KV_LORA), lambda bi, i: (0, 0)),
            pl.BlockSpec(w_kv.shape, lambda bi, i: (0, 0)),
            pl.BlockSpec((1, QK_NOPE), lambda bi, i: (0, 0)),
            pl.BlockSpec((1, LANES), lambda bi, i: (0, 0)),
            pl.BlockSpec((tm, LANES), lambda bi, i: (i, 0)),
        ],
        out_specs=[spec, spec],
        out_shape=[shape, shape],
        compiler_params=_cparams("parallel", "parallel"),
        name="kv_proj",
    )(zs, kv_lat_g.reshape(1, KV_LORA), w_kv, gn, gt, tab)


def _attn_kernel(q_ref, *refs, chunks):
    n_src = len(chunks)
    o_ref, acc_ref, m_ref = refs[2 * n_src:]
    q = q_ref[...]
    acc_ref[...] = jnp.zeros_like(acc_ref)
    m_ref[...] = jnp.full_like(m_ref, -jnp.inf)

    def step(k, v):
        s = lax.dot_general(q, k, (((1,), (1,)), ((), ())), preferred_element_type=F32)
        m_prev = m_ref[...]
        m_new = jnp.maximum(m_prev, jnp.max(s, axis=-1, keepdims=True))
        alpha = jnp.exp2(m_prev - m_new)
        p = jnp.exp2(s - jnp.concatenate([m_new] * (s.shape[1] // LANES), axis=1)).astype(BF16)
        pv = jnp.dot(p, v, preferred_element_type=F32)
        acc_ref[...] = acc_ref[...] * jnp.concatenate([alpha, alpha], axis=1) + pv
        m_ref[...] = m_new

    for si, (bk, n) in enumerate(chunks):
        k_ref, v_ref = refs[2 * si], refs[2 * si + 1]
        if n == 1:
            step(k_ref[...], v_ref[...])
        else:
            def body(c, carry, k_ref=k_ref, v_ref=v_ref, bk=bk):
                off = pl.multiple_of(c * bk, bk)
                step(k_ref[pl.ds(off, bk), :], v_ref[pl.ds(off, bk), :])
                return carry

            lax.fori_loop(0, n, body, 0, unroll=16)
    acc = acc_ref[...]
    o_ref[...] = (acc[:, :V_HEAD] / acc[:, V_HEAD:]).astype(o_ref.dtype)


def _attention(q, sources, *, bq, bk):
    b, h, l, _ = q.shape
    bq = min(bq, l)
    chunks = []
    in_specs = [pl.BlockSpec((None, None, bq, HEAD_PAD), lambda bi, hi, i: (bi, hi, i, 0))]
    args = [q]
    for k, v in sources:
        lk = k.shape[2]
        cb = min(bk, lk)
        chunks.append((cb, lk // cb))
        for a in (k, v):
            in_specs.append(pl.BlockSpec((None, None, lk, HEAD_PAD), lambda bi, hi, i: (bi, hi, 0, 0)))
            args.append(a)
    return pl.pallas_call(
        functools.partial(_attn_kernel, chunks=tuple(chunks)),
        grid=(b, h, l // bq),
        in_specs=in_specs,
        out_specs=pl.BlockSpec((None, bq, V_HEAD), lambda bi, hi, i: (bi, i, hi)),
        out_shape=jax.ShapeDtypeStruct((b, l, h * V_HEAD), BF16),
        scratch_shapes=[pltpu.VMEM((bq, HEAD_PAD), F32), pltpu.VMEM((bq, LANES), F32)],
        compiler_params=_cparams("parallel", "parallel", "arbitrary"),
        name="latent_attention",
    )(*args)


def _merge_out_kernel(p_ref, a_ref, s_ref, gp_ref, ga_ref, gs_ref, wp_ref, wa_ref, ws_ref, wo_ref, x_ref, g1_ref,
                      o_ref, y_ref, *, nj, tn):
    j = pl.program_id(2)

    @pl.when(j < nj)
    def _():
        y = gp_ref[...].astype(F32) * jnp.dot(p_ref[...], wp_ref[j], preferred_element_type=F32)
        y += ga_ref[...].astype(F32) * jnp.dot(a_ref[...], wa_ref[j], preferred_element_type=F32)
        y += gs_ref[...].astype(F32) * jnp.dot(s_ref[...], ws_ref[j], preferred_element_type=F32)
        y_ref[j] = y.astype(y_ref.dtype)

    @pl.when(j >= nj)
    def _():
        wo = wo_ref.at[j - nj]
        acc = jnp.dot(y_ref[0], wo[0:tn, :], preferred_element_type=F32)
        for c in range(1, nj):
            acc += jnp.dot(y_ref[c], wo[c * tn:(c + 1) * tn, :], preferred_element_type=F32)
        o_ref[...] = x_ref[...] + g1_ref[...] * acc


def _merge_out(pool_out, attn, sg_out, gates, w_p, w_a, w_s, w_o, layer, x, mod, *, tm, tn):
    b, l, _ = pool_out.shape
    nj = w_p.shape[1]
    d = nj * tn
    tm = min(tm, l)

    def first(j):
        return jnp.minimum(j, nj - 1)

    def second(j):
        return jnp.maximum(j - nj, 0)

    def act(width):
        return pl.BlockSpec((None, tm, width), lambda bi, i, j: (bi, i, 0))

    def gate(branch):
        return pl.BlockSpec((None, tm, tn), lambda bi, i, j: (bi, i, branch * nj + first(j)))

    def wgt(width):
        return pl.BlockSpec((None, nj, width, tn), lambda bi, i, j: (layer, 0, 0, 0), pipeline_mode=pl.Buffered(1))

    return pl.pallas_call(
        functools.partial(_merge_out_kernel, nj=nj, tn=tn),
        grid=(b, l // tm, 2 * nj),
        in_specs=[act(POOL_W), act(N_HEADS * V_HEAD), act(SG_W), gate(0), gate(1), gate(2),
                  wgt(POOL_W), wgt(N_HEADS * V_HEAD), wgt(SG_W), wgt(d),
                  pl.BlockSpec((None, tm, tn), lambda bi, i, j: (bi, i, second(j))),
                  pl.BlockSpec((None, 1, tn), lambda bi, i, j: (bi, 0, second(j)))],
        out_specs=pl.BlockSpec((None, tm, tn), lambda bi, i, j: (bi, i, second(j))),
        out_shape=jax.ShapeDtypeStruct((b, l, d), F32),
        scratch_shapes=[pltpu.VMEM((nj, tm, tn), BF16)],
        compiler_params=_cparams("parallel", "parallel", "arbitrary"),
        name="merge_out_proj",
    )(pool_out, attn, sg_out, gates, gates, gates, w_p, w_a, w_s, w_o, x, mod[:, 2:3, :])


def _mm_res_kernel(a_ref, w_ref, x_ref, gate_ref, o_ref):
    y = jnp.dot(a_ref[...], w_ref[...], preferred_element_type=F32)
    o_ref[...] = x_ref[...] + gate_ref[...] * y


def _mm_residual(a, w, layer, x, mod, gate_row, *, tm, tn, name):
    b, l, k = a.shape
    d = w.shape[2]
    tm = min(tm, l)
    return pl.pallas_call(
        _mm_res_kernel,
        grid=(b, l // tm, d // tn),
        in_specs=[
            pl.BlockSpec((None, tm, k), lambda bi, i, j: (bi, i, 0)),
            pl.BlockSpec((None, k, tn), lambda bi, i, j: (layer, 0, j)),
            pl.BlockSpec((None, tm, tn), lambda bi, i, j: (bi, i, j)),
            pl.BlockSpec((None, 1, tn), lambda bi, i, j: (bi, 0, j)),
        ],
        out_specs=pl.BlockSpec((None, tm, tn), lambda bi, i, j: (bi, i, j)),
        out_shape=jax.ShapeDtypeStruct((b, l, d), F32),
        compiler_params=_cparams("parallel", "parallel", "arbitrary"),
        name=name,
    )(a, w, x, mod[:, gate_row:gate_row + 1, :])


def _ffn_up_kernel(h_ref, wa_ref, wv_ref, cwa_ref, cwv_ref, cba_ref, cbv_ref, o_ref, *, tm):
    halo = BF16_ROWS
    h = h_ref[...]
    rows = h.shape[0]

    def conv(w_ref, cw_ref, cb_ref):
        u = jnp.dot(h, w_ref[...], preferred_element_type=F32)
        acc = cb_ref[...] + pltpu.roll(u, 1, axis=0) * cw_ref[0:1, :]
        acc = acc + u * cw_ref[1:2, :]
        acc = acc + pltpu.roll(u, rows - 1, axis=0) * cw_ref[2:3, :]
        return acc[halo:halo + tm, :]

    a = conv(wa_ref, cwa_ref, cba_ref)
    v = conv(wv_ref, cwv_ref, cbv_ref)
    o_ref[...] = (a * _sigmoid(a) * v).astype(o_ref.dtype)


def _ffn_up(h_tiles, w_up, layer, conv_w, conv_b, *, tf):
    b, nt, rows, d = h_tiles.shape
    tm = rows - 2 * BF16_ROWS
    l = nt * tm
    d_ff = w_up.shape[2] // 2
    nf = d_ff // tf
    return pl.pallas_call(
        functools.partial(_ffn_up_kernel, tm=tm),
        grid=(b, nt, nf),
        in_specs=[
            pl.BlockSpec((None, None, rows, d), lambda bi, i, j: (bi, i, 0, 0)),
            pl.BlockSpec((None, d, tf), lambda bi, i, j: (layer, 0, j)),
            pl.BlockSpec((None, d, tf), lambda bi, i, j: (layer, 0, nf + j)),
            pl.BlockSpec((CONV_W, tf), lambda bi, i, j: (0, j)),
            pl.BlockSpec((CONV_W, tf), lambda bi, i, j: (0, nf + j)),
            pl.BlockSpec((1, tf), lambda bi, i, j: (0, j)),
            pl.BlockSpec((1, tf), lambda bi, i, j: (0, nf + j)),
        ],
        out_specs=pl.BlockSpec((None, tm, tf), lambda bi, i, j: (bi, i, j)),
        out_shape=jax.ShapeDtypeStruct((b, l, d_ff), BF16),
        compiler_params=_cparams("parallel", "parallel", "arbitrary"),
        name="ffn_up_conv_gate",
    )(h_tiles, w_up, w_up, conv_w, conv_w, conv_b.reshape(1, -1), conv_b.reshape(1, -1))


def _rope_tables(n_tokens, rotate):
    if rotate:
        t = np.arange(n_tokens)
        inv_freq = ROPE_THETA ** (-np.arange(ROPE_FREQS, dtype=np.float64) / ROPE_FREQS)
        ar = (t // GRID_W)[:, None] * inv_freq
        ac = (t % GRID_W)[:, None] * inv_freq
        c = np.concatenate([np.cos(ar), np.cos(ar), np.cos(ac), np.cos(ac)], axis=1)
        s = np.concatenate([-np.sin(ar), np.sin(ar), -np.sin(ac), np.sin(ac)], axis=1)
    else:
        c, s = np.ones((n_tokens, QK_ROPE)), np.zeros((n_tokens, QK_ROPE))
    cat = lambda a, b_: jnp.asarray(np.concatenate([a, b_], axis=1), F32)
    return dict(q_cos=cat(c, c), q_sin=cat(s, s), k=cat(c, s))


def _stacked_weights(w_in, w_br_pool, w_br_mla, w_br_sg, w_o, ffn_up, ffn_down):
    depth, d, _ = w_in.shape
    o_q, o_kv, o_kr, o_sg, o_gate = POOL_W, POOL_W + Q_LORA, POOL_W + Q_LORA + KV_LORA, \
        POOL_W + Q_LORA + KV_LORA + QK_ROPE, POOL_W + Q_LORA + KV_LORA + QK_ROPE + 2 * SG_W
    w_kr = w_in[:, :, o_kr:o_sg]
    w_small = jnp.concatenate([
        w_in[:, :, :o_q], w_in[:, :, o_q:o_kv], w_in[:, :, o_sg:o_gate], w_in[:, :, o_kv:o_kr], w_kr,
        w_kr[:, :, ROPE_SWAP], jnp.zeros((depth, d, ZS_COLS - ZS_KV - KV_LORA - 2 * QK_ROPE), w_in.dtype)],
        axis=2).astype(BF16)
    def col_tiles(w):
        dep, k, n = w.shape
        return w.astype(BF16).reshape(dep, k, n // MERGE_TN, MERGE_TN).transpose(0, 2, 1, 3)

    return dict(
        w_small=w_small, w_gate=w_in[:, :, o_gate:].astype(BF16), w_br_pool=col_tiles(w_br_pool),
        w_br_mla=col_tiles(w_br_mla), w_br_sg=col_tiles(w_br_sg), w_o=col_tiles(w_o),
        ffn_up=ffn_up.astype(BF16), ffn_down=ffn_down.astype(BF16))


def _layer_weights(p):
    w_uq = p["w_uq"].reshape(Q_LORA, N_HEADS, QK_HEAD)
    w_q_rope = w_uq[:, :, QK_NOPE:]
    w_q = jnp.concatenate([
        w_uq[:, :, :QK_NOPE].reshape(Q_LORA, -1), w_q_rope.reshape(Q_LORA, -1),
        w_q_rope[:, :, ROPE_SWAP].reshape(Q_LORA, -1)], axis=1).astype(BF16)
    gq = p["q_norm_g"]
    gq_n = gq[:QK_NOPE].reshape(1, QK_NOPE)
    gq_r = jnp.tile(gq[QK_NOPE:], N_HEADS).reshape(1, -1)
    gq_rs = jnp.tile(gq[QK_NOPE:][ROPE_SWAP], N_HEADS).reshape(1, -1)

    w_ukv = p["w_ukv"].reshape(KV_LORA, N_HEADS, QK_NOPE + V_HEAD)
    w_kv = jnp.concatenate([
        w_ukv[:, :, :QK_NOPE].reshape(KV_LORA, -1), w_ukv[:, :, QK_NOPE:].reshape(KV_LORA, -1)], axis=1).astype(BF16)
    gk = p["k_norm_g"]
    gk_n = gk[:QK_NOPE].reshape(1, QK_NOPE)
    gk_t = jnp.concatenate([gk[QK_NOPE:], gk[QK_NOPE:][ROPE_SWAP]]).reshape(1, LANES)
    return dict(
        w_q=w_q, gq_n=gq_n, gq_r=gq_r, gq_rs=gq_rs, w_kv=w_kv, gk_n=gk_n, gk_t=gk_t,
        pool_w=p["pool_w"].astype(BF16), sg_w=p["sg_w"].astype(BF16))


def _keys_values(zs, p, w, tabs):
    return _kv_proj(zs, p["kv_lat_g"], w["w_kv"], w["gk_n"], w["gk_t"], tabs["k"], tm=512)


def _queries(zs, p, w, tabs):
    out_scale = QK_HEAD ** -0.5 * math.log2(math.e)
    return _q_proj(zs, p["q_lat_g"], w["w_q"], w["gq_n"], w["gq_r"], w["gq_rs"], tabs["q_cos"], tabs["q_sin"],
                   out_scale=out_scale, tm=512)


def _in_proj(x, p, ws, layer, mod, *, gates):
    h = _norm_modulate(x, p["norm1_g"], mod, shift_row=0, scale_row=1, tm=1024)
    zs = _matmul(h, ws["w_small"], layer, sigmoid=False, out_dtype=F32, tm=1024, tn=1280, name="in_proj_small")
    g = None
    if gates:
        g = _matmul(h, ws["w_gate"], layer, sigmoid=True, out_dtype=BF16, tm=1024, tn=1024, name="in_proj_gate")
    return zs, g


def _mix_and_ffn(x, zs, gates, attn, p, w, ws, layer, mod):
    pool_out = _pool_mix(zs, w["pool_w"], p["pool_scale"], tm=512)
    sg_out = _spatial_gating(zs, p["sg_norm_g"], w["sg_w"], p["sg_b"], tm=512)
    x = _merge_out(pool_out, attn, sg_out, gates, ws["w_br_pool"], ws["w_br_mla"], ws["w_br_sg"], ws["w_o"], layer,
                   x, mod, tm=1024, tn=MERGE_TN)
    h2 = _norm_modulate_tiles(x, p["norm2_g"], mod, shift_row=3, scale_row=4, tm=1024)
    act = _ffn_up(h2, ws["ffn_up"], layer, p["ffn_conv_w"], p["ffn_conv_b"], tf=512)
    return _mm_residual(act, ws["ffn_down"], layer, x, mod, 5, tm=1024, tn=512, name="ffn_down_residual")


def kernel(x, c, ctx, c_ctx, ada_w, ada_b, norm1_g, w_in, pool_w, pool_scale, q_lat_g, w_uq, kv_lat_g, w_ukv, q_norm_g, k_norm_g, sg_norm_g, sg_w, sg_b, w_br_pool, w_br_mla, w_br_sg, w_o, norm2_g, ffn_up, ffn_conv_w, ffn_conv_b, ffn_down):
    b, l, d = x.shape
    lc = ctx.shape[1]
    depth = ada_w.shape[0]
    assert b + 1 <= SUBLANES

    cvec = jnp.concatenate([c, c_ctx[None, :], jnp.zeros((SUBLANES - b - 1, d), c.dtype)], axis=0)
    mod = _ada(cvec, ada_w, ada_b).reshape(depth, SUBLANES, 6, d)
    rope_lat = _rope_tables(l, rotate=True)
    rope_ctx = _rope_tables(lc, rotate=False)
    ws = _stacked_weights(w_in, w_br_pool, w_br_mla, w_br_sg, w_o, ffn_up, ffn_down)

    xl, xc = x, ctx
    for i in range(depth):
        p = {
            "norm1_g": norm1_g[i], "pool_w": pool_w[i], "pool_scale": pool_scale[i],
            "q_lat_g": q_lat_g[i], "w_uq": w_uq[i], "kv_lat_g": kv_lat_g[i], "w_ukv": w_ukv[i],
            "q_norm_g": q_norm_g[i], "k_norm_g": k_norm_g[i], "sg_norm_g": sg_norm_g[i], "sg_w": sg_w[i],
            "sg_b": sg_b[i], "norm2_g": norm2_g[i], "ffn_conv_w": ffn_conv_w[i], "ffn_conv_b": ffn_conv_b[i],
        }
        w = _layer_weights(p)
        mod_lat = mod[i, :b]
        mod_ctx = jnp.broadcast_to(mod[i, b:b + 1], (b, 6, d))

        zl, gl = _in_proj(xl, p, ws, i, mod_lat, gates=True)
        zc, gc = _in_proj(xc, p, ws, i, mod_ctx, gates=i < depth - 1)
        kc, vc = _keys_values(zc, p, w, rope_ctx)
        kl, vl = _keys_values(zl, p, w, rope_lat)
        ql = _queries(zl, p, w, rope_lat)
        attn_l = _attention(ql, [(kl, vl), (kc, vc)], bq=2048, bk=512)
        xl = _mix_and_ffn(xl, zl, gl, attn_l, p, w, ws, i, mod_lat)
        if i < depth - 1:
            qc = _queries(zc, p, w, rope_ctx)
            attn_c = _attention(qc, [(kc, vc)], bq=512, bk=512)
            xc = _mix_and_ffn(xc, zc, gc, attn_c, p, w, ws, i, mod_ctx)
    return xl
```

```python
import functools
import math

import jax
import jax.numpy as jnp
import numpy as np
from jax import lax
from jax.experimental import pallas as pl
from jax.experimental.pallas import tpu as pltpu

F32 = jnp.float32
BF16 = jnp.bfloat16

GRID_W = 64
EPS = 1e-6
POOL_W = 512
POOL_WINDOWS = (2, 4, 8, 16)
POOL_GROUP_W = POOL_W // len(POOL_WINDOWS)
N_HEADS = 8
Q_LORA = 512
KV_LORA = 256
QK_NOPE = 128
QK_ROPE = 64
QK_HEAD = QK_NOPE + QK_ROPE
V_HEAD = 128
ROPE_FREQS = QK_ROPE // 4
ROPE_THETA = 10000.0
SG_W = 512
SG_GROUPS = 4
SG_GROUP_W = SG_W // SG_GROUPS
CHUNK = 128
CONV_W = 3

LANES = 128
SUBLANES = 8
BF16_ROWS = 16
HEAD_PAD = 2 * LANES

VMEM_LIMIT_BYTES = 56 * 1024 * 1024
ADA_TN = 1024
DENSE_TM = 1024
MIXER_TM = 1024
SMALL_TN = 1280
GATE_TN = 1024
MERGE_TN = 512
FFN_TF = 512
DOWN_TN = 512
ATT_BQ = 2048
ATT_BK = 512
ATT_UNROLL = 16

ZS_POOL = 0
ZS_Q = 512
ZS_SG = 1024
ZS_KV = 2048
ZS_COLS = 2560
ROPE_SWAP = np.concatenate([np.arange(16, 32), np.arange(0, 16), np.arange(48, 64), np.arange(32, 48)])


def _cparams(*sem):
    return pltpu.CompilerParams(dimension_semantics=sem, vmem_limit_bytes=VMEM_LIMIT_BYTES)


def _rms_scale(x, width):
    return lax.rsqrt(jnp.sum(x * x, axis=-1, keepdims=True) * (1.0 / width) + EPS)


def _ada_kernel(c_ref, w_ref, b_ref, o_ref):
    c = c_ref[...]
    s = c * jax.nn.sigmoid(c)
    o_ref[...] = jnp.dot(s, w_ref[...], preferred_element_type=F32, precision=lax.Precision.HIGHEST) + b_ref[...]


def _ada(cvec, ada_w, ada_b):
    depth, d, n = ada_w.shape
    tn = ADA_TN
    return pl.pallas_call(
        _ada_kernel,
        grid=(depth, n // tn),
        in_specs=[
            pl.BlockSpec((SUBLANES, d), lambda l, j: (0, 0)),
            pl.BlockSpec((None, d, tn), lambda l, j: (l, 0, j)),
            pl.BlockSpec((None, 1, tn), lambda l, j: (l, 0, j)),
        ],
        out_specs=pl.BlockSpec((None, SUBLANES, tn), lambda l, j: (l, 0, j)),
        out_shape=jax.ShapeDtypeStruct((depth, SUBLANES, n), F32),
        compiler_params=_cparams("parallel", "arbitrary"),
        name="ada_modulation",
    )(cvec, ada_w, ada_b.reshape(depth, 1, n))


def _norm_mod(x, g, mod_ref, shift_row, scale_row):
    y = x * _rms_scale(x, x.shape[-1]) * g
    return y * (1.0 + mod_ref[scale_row:scale_row + 1, :]) + mod_ref[shift_row:shift_row + 1, :]


def _sigmoid(x):
    return 0.5 * jnp.tanh(0.5 * x) + 0.5


def _norm_kernel(x_ref, g_ref, mod_ref, o_ref, *, shift_row, scale_row):
    o_ref[...] = _norm_mod(x_ref[...], g_ref[...], mod_ref, shift_row, scale_row).astype(o_ref.dtype)


def _norm_modulate(x, g, mod, *, shift_row, scale_row, tm):
    b, l, d = x.shape
    tm = min(tm, l)
    return pl.pallas_call(
        functools.partial(_norm_kernel, shift_row=shift_row, scale_row=scale_row),
        grid=(b, l // tm),
        in_specs=[
            pl.BlockSpec((None, tm, d), lambda bi, i: (bi, i, 0)),
            pl.BlockSpec((1, d), lambda bi, i: (0, 0)),
            pl.BlockSpec((None, 6, d), lambda bi, i: (bi, 0, 0)),
        ],
        out_specs=pl.BlockSpec((None, tm, d), lambda bi, i: (bi, i, 0)),
        out_shape=jax.ShapeDtypeStruct((b, l, d), BF16),
        compiler_params=_cparams("parallel", "parallel"),
        name="norm_modulate",
    )(x, g.reshape(1, d), mod)


def _norm_halo_kernel(xp_ref, xc_ref, xn_ref, g_ref, mod_ref, o_ref, *, tm, nt, shift_row, scale_row):
    ti = pl.program_id(1)
    halo = BF16_ROWS
    g = g_ref[...]
    o_ref[halo:halo + tm, :] = _norm_mod(xc_ref[...], g, mod_ref, shift_row, scale_row).astype(o_ref.dtype)
    prev = jnp.where(ti > 0, _norm_mod(xp_ref[...], g, mod_ref, shift_row, scale_row), 0.0)
    o_ref[0:halo, :] = prev.astype(o_ref.dtype)
    nxt = jnp.where(ti < nt - 1, _norm_mod(xn_ref[...], g, mod_ref, shift_row, scale_row), 0.0)
    o_ref[halo + tm:, :] = nxt.astype(o_ref.dtype)


def _norm_modulate_tiles(x, g, mod, *, shift_row, scale_row, tm):
    b, l, d = x.shape
    tm = min(tm, l)
    nt = l // tm
    hb = tm // BF16_ROWS
    rows = tm + 2 * BF16_ROWS
    return pl.pallas_call(
        functools.partial(_norm_halo_kernel, tm=tm, nt=nt, shift_row=shift_row, scale_row=scale_row),
        grid=(b, nt),
        in_specs=[
            pl.BlockSpec((None, BF16_ROWS, d), lambda bi, i: (bi, jnp.maximum(i * hb - 1, 0), 0)),
            pl.BlockSpec((None, tm, d), lambda bi, i: (bi, i, 0)),
            pl.BlockSpec((None, BF16_ROWS, d), lambda bi, i: (bi, jnp.minimum((i + 1) * hb, l // BF16_ROWS - 1), 0)),
            pl.BlockSpec((1, d), lambda bi, i: (0, 0)),
            pl.BlockSpec((None, 6, d), lambda bi, i: (bi, 0, 0)),
        ],
        out_specs=pl.BlockSpec((None, None, rows, d), lambda bi, i: (bi, i, 0, 0)),
        out_shape=jax.ShapeDtypeStruct((b, nt, rows, d), BF16),
        compiler_params=_cparams("parallel", "parallel"),
        name="norm_modulate_tiles",
    )(x, x, x, g.reshape(1, d), mod)


def _matmul_kernel(a_ref, w_ref, o_ref, *, sigmoid):
    z = jnp.dot(a_ref[...], w_ref[...], preferred_element_type=F32)
    if sigmoid:
        z = _sigmoid(z)
    o_ref[...] = z.astype(o_ref.dtype)


def _matmul(a, w, layer, *, sigmoid, out_dtype, tm, tn, name):
    b, l, k = a.shape
    n = w.shape[2]
    tm = min(tm, l)
    return pl.pallas_call(
        functools.partial(_matmul_kernel, sigmoid=sigmoid),
        grid=(b, l // tm, n // tn),
        in_specs=[
            pl.BlockSpec((None, tm, k), lambda bi, i, j: (bi, i, 0)),
            pl.BlockSpec((None, k, tn), lambda bi, i, j: (layer, 0, j)),
        ],
        out_specs=pl.BlockSpec((None, tm, tn), lambda bi, i, j: (bi, i, j)),
        out_shape=jax.ShapeDtypeStruct((b, l, n), out_dtype),
        compiler_params=_cparams("parallel", "parallel", "arbitrary"),
        name=name,
    )(a, w)


def _pool_kernel(prev_ref, cur_ref, next_ref, pw_ref, ps_ref, o_ref, ext_ref, *, tm, nt, seq_len):
    ti = pl.program_id(1)
    cur = cur_ref[...]
    ext_ref[SUBLANES:SUBLANES + tm, :] = cur
    ext_ref[0:SUBLANES, :] = jnp.where(ti > 0, prev_ref[...], 0.0)
    ext_ref[SUBLANES + tm:, :] = jnp.where(ti < nt - 1, next_ref[...], 0.0)
    t = ti * tm + lax.broadcasted_iota(jnp.int32, (tm, 1), 0)
    for g, w in enumerate(POOL_WINDOWS):
        r = w // 2
        c0 = g * POOL_GROUP_W
        s = ext_ref[SUBLANES - r:SUBLANES - r + tm, c0:c0 + POOL_GROUP_W]
        for dlt in range(-r + 1, r):
            s = s + ext_ref[SUBLANES + dlt:SUBLANES + dlt + tm, c0:c0 + POOL_GROUP_W]
        cnt = (jnp.minimum(t + r, seq_len) - jnp.maximum(t - r, 0)).astype(F32)
        pooled = s / cnt - cur[:, c0:c0 + POOL_GROUP_W]
        y = jnp.dot(pooled.astype(BF16), pw_ref[g], preferred_element_type=F32)
        o_ref[:, c0:c0 + POOL_GROUP_W] = (y * ps_ref[:, c0:c0 + POOL_GROUP_W]).astype(o_ref.dtype)


def _pool_mix(zs, pool_w, pool_scale, *, tm):
    b, l, _ = zs.shape
    tm = min(tm, l)
    nt = l // tm
    hb = tm // SUBLANES
    col = ZS_POOL // POOL_W
    return pl.pallas_call(
        functools.partial(_pool_kernel, tm=tm, nt=nt, seq_len=l),
        grid=(b, nt),
        in_specs=[
            pl.BlockSpec((None, SUBLANES, POOL_W), lambda bi, i: (bi, jnp.maximum(i * hb - 1, 0), col)),
            pl.BlockSpec((None, tm, POOL_W), lambda bi, i: (bi, i, col)),
            pl.BlockSpec((None, SUBLANES, POOL_W),
                         lambda bi, i: (bi, jnp.minimum((i + 1) * hb, l // SUBLANES - 1), col)),
            pl.BlockSpec((len(POOL_WINDOWS), POOL_GROUP_W, POOL_GROUP_W), lambda bi, i: (0, 0, 0)),
            pl.BlockSpec((1, POOL_W), lambda bi, i: (0, 0)),
        ],
        out_specs=pl.BlockSpec((None, tm, POOL_W), lambda bi, i: (bi, i, 0)),
        out_shape=jax.ShapeDtypeStruct((b, l, POOL_W), BF16),
        scratch_shapes=[pltpu.VMEM((tm + 2 * SUBLANES, POOL_W), F32)],
        compiler_params=_cparams("parallel", "parallel"),
        name="pool_mix",
    )(zs, zs, zs, pool_w, pool_scale.reshape(1, POOL_W))


def _sg_kernel(z_ref, g_ref, w_ref, b_ref, o_ref, *, tm):
    z = z_ref[...]
    a = z * (0.5 * (1.0 + jnp.tanh(math.sqrt(2.0 / math.pi) * (z + 0.044715 * (z * z * z)))))
    u = a[:, :SG_W]
    v = a[:, SG_W:]
    vc = v - jnp.mean(v, axis=-1, keepdims=True)
    vn = (vc * lax.rsqrt(jnp.mean(vc * vc, axis=-1, keepdims=True) + EPS) * g_ref[...]).astype(BF16)
    for c in range(tm // CHUNK):
        rows = slice(c * CHUNK, (c + 1) * CHUNK)
        for g in range(SG_GROUPS):
            cols = slice(g * SG_GROUP_W, (g + 1) * SG_GROUP_W)
            s = jnp.dot(w_ref[g], vn[rows, cols], preferred_element_type=F32) + b_ref[g]
            o_ref[rows, cols] = (u[rows, cols] * s).astype(o_ref.dtype)


def _spatial_gating(zs, sg_norm_g, sg_w, sg_b, *, tm):
    b, l, _ = zs.shape
    tm = min(tm, l)
    return pl.pallas_call(
        functools.partial(_sg_kernel, tm=tm),
        grid=(b, l // tm),
        in_specs=[
            pl.BlockSpec((None, tm, 2 * SG_W), lambda bi, i: (bi, i, ZS_SG // (2 * SG_W))),
            pl.BlockSpec((1, SG_W), lambda bi, i: (0, 0)),
            pl.BlockSpec((SG_GROUPS, CHUNK, CHUNK), lambda bi, i: (0, 0, 0)),
            pl.BlockSpec((SG_GROUPS, CHUNK, 1), lambda bi, i: (0, 0, 0)),
        ],
        out_specs=pl.BlockSpec((None, tm, SG_W), lambda bi, i: (bi, i, 0)),
        out_shape=jax.ShapeDtypeStruct((b, l, SG_W), BF16),
        compiler_params=_cparams("parallel", "parallel"),
        name="spatial_gating",
    )(zs, sg_norm_g.reshape(1, SG_W), sg_w, sg_b.reshape(SG_GROUPS, CHUNK, 1))


def _q_kernel(z_ref, gl_ref, w_ref, gn_ref, gr_ref, grs_ref, cos_ref, sin_ref, o_ref, *, out_scale):
    z = z_ref[...]
    zn = (z * _rms_scale(z, Q_LORA) * gl_ref[...]).astype(BF16)
    q = jnp.dot(zn, w_ref[...], preferred_element_type=F32)
    nope_w = N_HEADS * QK_NOPE
    rope_w = N_HEADS * QK_ROPE
    lo = lax.broadcasted_iota(jnp.int32, (1, LANES), 1) < QK_ROPE
    cos = cos_ref[...]
    sin = sin_ref[...]
    gn = gn_ref[...] * out_scale
    for pair in range(N_HEADS // 2):
        blk = slice(pair * LANES, (pair + 1) * LANES)
        r2 = q[:, nope_w + pair * LANES:nope_w + (pair + 1) * LANES]
        r2s = q[:, nope_w + rope_w + pair * LANES:nope_w + rope_w + (pair + 1) * LANES]
        sq = r2 * r2
        ss_lo = jnp.sum(jnp.where(lo, sq, 0.0), axis=-1, keepdims=True)
        ss_hi = jnp.sum(jnp.where(lo, 0.0, sq), axis=-1, keepdims=True)
        rot = r2 * gr_ref[:, blk] * cos + r2s * grs_ref[:, blk] * sin
        for half, ss_r in enumerate((ss_lo, ss_hi)):
            h = 2 * pair + half
            nope = q[:, h * QK_NOPE:(h + 1) * QK_NOPE]
            inv = lax.rsqrt((jnp.sum(nope * nope, axis=-1, keepdims=True) + ss_r) * (1.0 / QK_HEAD) + EPS)
            o_ref[h, :, 0:LANES] = (nope * inv * gn).astype(o_ref.dtype)
            keep = lo if half == 0 else jnp.logical_not(lo)
            o_ref[h, :, LANES:HEAD_PAD] = jnp.where(keep, rot * (inv * out_scale), 0.0).astype(o_ref.dtype)


def _q_proj(zs, q_lat_g, w_q, gn, gr, grs, cos_tab, sin_tab, *, out_scale, tm):
    b, l, _ = zs.shape
    tm = min(tm, l)
    return pl.pallas_call(
        functools.partial(_q_kernel, out_scale=out_scale),
        grid=(b, l // tm),
        in_specs=[
            pl.BlockSpec((None, tm, Q_LORA), lambda bi, i: (bi, i, ZS_Q // Q_LORA)),
            pl.BlockSpec((1, Q_LORA), lambda bi, i: (0, 0)),
            pl.BlockSpec(w_q.shape, lambda bi, i: (0, 0)),
            pl.BlockSpec((1, QK_NOPE), lambda bi, i: (0, 0)),
            pl.BlockSpec((1, N_HEADS * QK_ROPE), lambda bi, i: (0, 0)),
            pl.BlockSpec((1, N_HEADS * QK_ROPE), lambda bi, i: (0, 0)),
            pl.BlockSpec((tm, LANES), lambda bi, i: (i, 0)),
            pl.BlockSpec((tm, LANES), lambda bi, i: (i, 0)),
        ],
        out_specs=pl.BlockSpec((None, N_HEADS, tm, HEAD_PAD), lambda bi, i: (bi, 0, i, 0)),
        out_shape=jax.ShapeDtypeStruct((b, N_HEADS, l, HEAD_PAD), BF16),
        compiler_params=_cparams("parallel", "parallel"),
        name="q_proj",
    )(zs, q_lat_g.reshape(1, Q_LORA), w_q, gn, gr, grs, cos_tab, sin_tab)


def _kv_kernel(z_ref, gl_ref, w_ref, gn_ref, gt_ref, tab_ref, k_ref, v_ref):
    z = z_ref[...]
    zkv = z[:, :KV_LORA]
    zn = (zkv * _rms_scale(zkv, KV_LORA) * gl_ref[...]).astype(BF16)
    kv = jnp.dot(zn, w_ref[...], preferred_element_type=F32)
    lo = lax.broadcasted_iota(jnp.int32, (1, LANES), 1) < QK_ROPE
    kr2 = z[:, KV_LORA:KV_LORA + LANES]
    ss_r = jnp.sum(jnp.where(lo, kr2 * kr2, 0.0), axis=-1, keepdims=True)
    t = kr2 * gt_ref[...] * tab_ref[...]
    rot = t + pltpu.roll(t, QK_ROPE, axis=1)
    gn = gn_ref[...]
    ones = jnp.ones((z.shape[0], LANES), v_ref.dtype)
    for h in range(N_HEADS):
        nope = kv[:, h * QK_NOPE:(h + 1) * QK_NOPE]
        inv = lax.rsqrt((jnp.sum(nope * nope, axis=-1, keepdims=True) + ss_r) * (1.0 / QK_HEAD) + EPS)
        k_ref[h, :, 0:LANES] = (nope * inv * gn).astype(k_ref.dtype)
        keep = lo if h % 2 == 0 else jnp.logical_not(lo)
        k_ref[h, :, LANES:HEAD_PAD] = jnp.where(keep, rot * inv, 0.0).astype(k_ref.dtype)
        v_ref[h, :, 0:LANES] = kv[:, N_HEADS * QK_NOPE + h * V_HEAD:N_HEADS * QK_NOPE + (h + 1) * V_HEAD].astype(
            v_ref.dtype)
        v_ref[h, :, LANES:HEAD_PAD] = ones


def _kv_proj(zs, kv_lat_g, w_kv, gn, gt, tab, *, tm):
    b, l, _ = zs.shape
    tm = min(tm, l)
    blk = KV_LORA + 2 * LANES
    shape = jax.ShapeDtypeStruct((b, N_HEADS, l, HEAD_PAD), BF16)
    spec = pl.BlockSpec((None, N_HEADS, tm, HEAD_PAD), lambda bi, i: (bi, 0, i, 0))
    return pl.pallas_call(
        _kv_kernel,
        grid=(b, l // tm),
        in_specs=[
            pl.BlockSpec((None, tm, blk), lambda bi, i: (bi, i, ZS_KV // blk)),
            pl.BlockSpec((1, KV_LORA), lambda bi, i: (0, 0)),
            pl.BlockSpec(w_kv.shape, lambda bi, i: (0, 0)),
            pl.BlockSpec((1, QK_NOPE), lambda bi, i: (0, 0)),
            pl.BlockSpec((1, LANES), lambda bi, i: (0, 0)),
            pl.BlockSpec((tm, LANES), lambda bi, i: (i, 0)),
        ],
        out_specs=[spec, spec],
        out_shape=[shape, shape],
        compiler_params=_cparams("parallel", "parallel"),
        name="kv_proj",
    )(zs, kv_lat_g.reshape(1, KV_LORA), w_kv, gn, gt, tab)


def _attn_kernel(q_ref, *refs, chunks):
    n_src = len(chunks)
    o_ref, acc_ref, m_ref = refs[2 * n_src:]
    q = q_ref[...]
    acc_ref[...] = jnp.zeros_like(acc_ref)
    m_ref[...] = jnp.full_like(m_ref, -jnp.inf)

    def step(k, v):
        s = lax.dot_general(q, k, (((1,), (1,)), ((), ())), preferred_element_type=F32)
        m_prev = m_ref[...]
        m_new = jnp.maximum(m_prev, jnp.max(s, axis=-1, keepdims=True))
        alpha = jnp.exp2(m_prev - m_new)
        p = jnp.exp2(s - jnp.concatenate([m_new] * (s.shape[1] // LANES), axis=1)).astype(BF16)
        pv = jnp.dot(p, v, preferred_element_type=F32)
        acc_ref[...] = acc_ref[...] * jnp.concatenate([alpha, alpha], axis=1) + pv
        m_ref[...] = m_new

    for si, (bk, n) in enumerate(chunks):
        k_ref, v_ref = refs[2 * si], refs[2 * si + 1]
        if n == 1:
            step(k_ref[...], v_ref[...])
        else:
            def body(c, carry, k_ref=k_ref, v_ref=v_ref, bk=bk):
                off = pl.multiple_of(c * bk, bk)
                step(k_ref[pl.ds(off, bk), :], v_ref[pl.ds(off, bk), :])
                return carry

            lax.fori_loop(0, n, body, 0, unroll=ATT_UNROLL)
    acc = acc_ref[...]
    o_ref[...] = (acc[:, :V_HEAD] / acc[:, V_HEAD:]).astype(o_ref.dtype)


def _attention(q, sources, *, bq, bk):
    b, h, l, _ = q.shape
    bq = min(bq, l)
    chunks = []
    in_specs = [pl.BlockSpec((None, None, bq, HEAD_PAD), lambda bi, hi, i: (bi, hi, i, 0))]
    args = [q]
    for k, v in sources:
        lk = k.shape[2]
        cb = min(bk, lk)
        chunks.append((cb, lk // cb))
        for a in (k, v):
            in_specs.append(pl.BlockSpec((None, None, lk, HEAD_PAD), lambda bi, hi, i: (bi, hi, 0, 0)))
            args.append(a)
    return pl.pallas_call(
        functools.partial(_attn_kernel, chunks=tuple(chunks)),
        grid=(b, h, l // bq),
        in_specs=in_specs,
        out_specs=pl.BlockSpec((None, bq, V_HEAD), lambda bi, hi, i: (bi, i, hi)),
        out_shape=jax.ShapeDtypeStruct((b, l, h * V_HEAD), BF16),
        scratch_shapes=[pltpu.VMEM((bq, HEAD_PAD), F32), pltpu.VMEM((bq, LANES), F32)],
        compiler_params=_cparams("parallel", "parallel", "arbitrary"),
        name="latent_attention",
    )(*args)


def _merge_out_kernel(p_ref, a_ref, s_ref, gp_ref, ga_ref, gs_ref, wp_ref, wa_ref, ws_ref, wo_ref, x_ref, g1_ref,
                      o_ref, y_ref, *, nj, tn):
    j = pl.program_id(2)

    @pl.when(j < nj)
    def _():
        y = gp_ref[...].astype(F32) * jnp.dot(p_ref[...], wp_ref[j], preferred_element_type=F32)
        y += ga_ref[...].astype(F32) * jnp.dot(a_ref[...], wa_ref[j], preferred_element_type=F32)
        y += gs_ref[...].astype(F32) * jnp.dot(s_ref[...], ws_ref[j], preferred_element_type=F32)
        y_ref[j] = y.astype(y_ref.dtype)

    @pl.when(j >= nj)
    def _():
        wo = wo_ref.at[j - nj]
        acc = jnp.dot(y_ref[0], wo[0:tn, :], preferred_element_type=F32)
        for c in range(1, nj):
            acc += jnp.dot(y_ref[c], wo[c * tn:(c + 1) * tn, :], preferred_element_type=F32)
        o_ref[...] = x_ref[...] + g1_ref[...] * acc


def _merge_out(pool_out, attn, sg_out, gates, w_p, w_a, w_s, w_o, layer, x, mod, *, tm, tn):
    b, l, _ = pool_out.shape
    nj = w_p.shape[1]
    d = nj * tn
    tm = min(tm, l)

    def first(j):
        return jnp.minimum(j, nj - 1)

    def second(j):
        return jnp.maximum(j - nj, 0)

    def act(width):
        return pl.BlockSpec((None, tm, width), lambda bi, i, j: (bi, i, 0))

    def gate(branch):
        return pl.BlockSpec((None, tm, tn), lambda bi, i, j: (bi, i, branch * nj + first(j)))

    def wgt(width):
        return pl.BlockSpec((None, nj, width, tn), lambda bi, i, j: (layer, 0, 0, 0), pipeline_mode=pl.Buffered(1))

    return pl.pallas_call(
        functools.partial(_merge_out_kernel, nj=nj, tn=tn),
        grid=(b, l // tm, 2 * nj),
        in_specs=[act(POOL_W), act(N_HEADS * V_HEAD), act(SG_W), gate(0), gate(1), gate(2),
                  wgt(POOL_W), wgt(N_HEADS * V_HEAD), wgt(SG_W), wgt(d),
                  pl.BlockSpec((None, tm, tn), lambda bi, i, j: (bi, i, second(j))),
                  pl.BlockSpec((None, 1, tn), lambda bi, i, j: (bi, 0, second(j)))],
        out_specs=pl.BlockSpec((None, tm, tn), lambda bi, i, j: (bi, i, second(j))),
        out_shape=jax.ShapeDtypeStruct((b, l, d), F32),
        scratch_shapes=[pltpu.VMEM((nj, tm, tn), BF16)],
        compiler_params=_cparams("parallel", "parallel", "arbitrary"),
        name="merge_out_proj",
    )(pool_out, attn, sg_out, gates, gates, gates, w_p, w_a, w_s, w_o, x, mod[:, 2:3, :])


def _mm_res_kernel(a_ref, w_ref, x_ref, gate_ref, o_ref):
    y = jnp.dot(a_ref[...], w_ref[...], preferred_element_type=F32)
    o_ref[...] = x_ref[...] + gate_ref[...] * y


def _mm_residual(a, w, layer, x, mod, gate_row, *, tm, tn, name):
    b, l, k = a.shape
    d = w.shape[2]
    tm = min(tm, l)
    return pl.pallas_call(
        _mm_res_kernel,
        grid=(b, l // tm, d // tn),
        in_specs=[
            pl.BlockSpec((None, tm, k), lambda bi, i, j: (bi, i, 0)),
            pl.BlockSpec((None, k, tn), lambda bi, i, j: (layer, 0, j)),
            pl.BlockSpec((None, tm, tn), lambda bi, i, j: (bi, i, j)),
            pl.BlockSpec((None, 1, tn), lambda bi, i, j: (bi, 0, j)),
        ],
        out_specs=pl.BlockSpec((None, tm, tn), lambda bi, i, j: (bi, i, j)),
        out_shape=jax.ShapeDtypeStruct((b, l, d), F32),
        compiler_params=_cparams("parallel", "parallel", "arbitrary"),
        name=name,
    )(a, w, x, mod[:, gate_row:gate_row + 1, :])


def _ffn_up_kernel(h_ref, wa_ref, wv_ref, cwa_ref, cwv_ref, cba_ref, cbv_ref, o_ref, *, tm):
    halo = BF16_ROWS
    h = h_ref[...]
    rows = h.shape[0]

    def conv(w_ref, cw_ref, cb_ref):
        u = jnp.dot(h, w_ref[...], preferred_element_type=F32)
        acc = cb_ref[...] + pltpu.roll(u, 1, axis=0) * cw_ref[0:1, :]
        acc = acc + u * cw_ref[1:2, :]
        acc = acc + pltpu.roll(u, rows - 1, axis=0) * cw_ref[2:3, :]
        return acc[halo:halo + tm, :]

    a = conv(wa_ref, cwa_ref, cba_ref)
    v = conv(wv_ref, cwv_ref, cbv_ref)
    o_ref[...] = (a * _sigmoid(a) * v).astype(o_ref.dtype)


def _ffn_up(h_tiles, w_up, layer, conv_w, conv_b, *, tf):
    b, nt, rows, d = h_tiles.shape
    tm = rows - 2 * BF16_ROWS
    l = nt * tm
    d_ff = w_up.shape[2] // 2
    nf = d_ff // tf
    return pl.pallas_call(
        functools.partial(_ffn_up_kernel, tm=tm),
        grid=(b, nt, nf),
        in_specs=[
            pl.BlockSpec((None, None, rows, d), lambda bi, i, j: (bi, i, 0, 0)),
            pl.BlockSpec((None, d, tf), lambda bi, i, j: (layer, 0, j)),
            pl.BlockSpec((None, d, tf), lambda bi, i, j: (layer, 0, nf + j)),
            pl.BlockSpec((CONV_W, tf), lambda bi, i, j: (0, j)),
            pl.BlockSpec((CONV_W, tf), lambda bi, i, j: (0, nf + j)),
            pl.BlockSpec((1, tf), lambda bi, i, j: (0, j)),
            pl.BlockSpec((1, tf), lambda bi, i, j: (0, nf + j)),
        ],
        out_specs=pl.BlockSpec((None, tm, tf), lambda bi, i, j: (bi, i, j)),
        out_shape=jax.ShapeDtypeStruct((b, l, d_ff), BF16),
        compiler_params=_cparams("parallel", "parallel", "arbitrary"),
        name="ffn_up_conv_gate",
    )(h_tiles, w_up, w_up, conv_w, conv_w, conv_b.reshape(1, -1), conv_b.reshape(1, -1))


def _rope_tables(n_tokens, rotate):
    if rotate:
        t = np.arange(n_tokens)
        inv_freq = ROPE_THETA ** (-np.arange(ROPE_FREQS, dtype=np.float64) / ROPE_FREQS)
        ar = (t // GRID_W)[:, None] * inv_freq
        ac = (t % GRID_W)[:, None] * inv_freq
        c = np.concatenate([np.cos(ar), np.cos(ar), np.cos(ac), np.cos(ac)], axis=1)
        s = np.concatenate([-np.sin(ar), np.sin(ar), -np.sin(ac), np.sin(ac)], axis=1)
    else:
        c, s = np.ones((n_tokens, QK_ROPE)), np.zeros((n_tokens, QK_ROPE))
    cat = lambda a, b_: jnp.asarray(np.concatenate([a, b_], axis=1), F32)
    return dict(q_cos=cat(c, c), q_sin=cat(s, s), k=cat(c, s))


def _stacked_weights(w_in, w_br_pool, w_br_mla, w_br_sg, w_o, ffn_up, ffn_down):
    depth, d, _ = w_in.shape
    o_q, o_kv, o_kr, o_sg, o_gate = POOL_W, POOL_W + Q_LORA, POOL_W + Q_LORA + KV_LORA, \
        POOL_W + Q_LORA + KV_LORA + QK_ROPE, POOL_W + Q_LORA + KV_LORA + QK_ROPE + 2 * SG_W
    w_kr = w_in[:, :, o_kr:o_sg]
    w_small = jnp.concatenate([
        w_in[:, :, :o_q], w_in[:, :, o_q:o_kv], w_in[:, :, o_sg:o_gate], w_in[:, :, o_kv:o_kr], w_kr,
        w_kr[:, :, ROPE_SWAP], jnp.zeros((depth, d, ZS_COLS - ZS_KV - KV_LORA - 2 * QK_ROPE), w_in.dtype)],
        axis=2).astype(BF16)
    def col_tiles(w):
        dep, k, n = w.shape
        return w.astype(BF16).reshape(dep, k, n // MERGE_TN, MERGE_TN).transpose(0, 2, 1, 3)

    return dict(
        w_small=w_small, w_gate=w_in[:, :, o_gate:].astype(BF16), w_br_pool=col_tiles(w_br_pool),
        w_br_mla=col_tiles(w_br_mla), w_br_sg=col_tiles(w_br_sg), w_o=col_tiles(w_o),
        ffn_up=ffn_up.astype(BF16), ffn_down=ffn_down.astype(BF16))


def _layer_weights(p):
    w_uq = p["w_uq"].reshape(Q_LORA, N_HEADS, QK_HEAD)
    w_q_rope = w_uq[:, :, QK_NOPE:]
    w_q = jnp.concatenate([
        w_uq[:, :, :QK_NOPE].reshape(Q_LORA, -1), w_q_rope.reshape(Q_LORA, -1),
        w_q_rope[:, :, ROPE_SWAP].reshape(Q_LORA, -1)], axis=1).astype(BF16)
    gq = p["q_norm_g"]
    gq_n = gq[:QK_NOPE].reshape(1, QK_NOPE)
    gq_r = jnp.tile(gq[QK_NOPE:], N_HEADS).reshape(1, -1)
    gq_rs = jnp.tile(gq[QK_NOPE:][ROPE_SWAP], N_HEADS).reshape(1, -1)

    w_ukv = p["w_ukv"].reshape(KV_LORA, N_HEADS, QK_NOPE + V_HEAD)
    w_kv = jnp.concatenate([
        w_ukv[:, :, :QK_NOPE].reshape(KV_LORA, -1), w_ukv[:, :, QK_NOPE:].reshape(KV_LORA, -1)], axis=1).astype(BF16)
    gk = p["k_norm_g"]
    gk_n = gk[:QK_NOPE].reshape(1, QK_NOPE)
    gk_t = jnp.concatenate([gk[QK_NOPE:], gk[QK_NOPE:][ROPE_SWAP]]).reshape(1, LANES)
    return dict(
        w_q=w_q, gq_n=gq_n, gq_r=gq_r, gq_rs=gq_rs, w_kv=w_kv, gk_n=gk_n, gk_t=gk_t,
        pool_w=p["pool_w"].astype(BF16), sg_w=p["sg_w"].astype(BF16))


def _keys_values(zs, p, w, tabs):
    return _kv_proj(zs, p["kv_lat_g"], w["w_kv"], w["gk_n"], w["gk_t"], tabs["k"], tm=MIXER_TM)


def _queries(zs, p, w, tabs):
    out_scale = QK_HEAD ** -0.5 * math.log2(math.e)
    return _q_proj(zs, p["q_lat_g"], w["w_q"], w["gq_n"], w["gq_r"], w["gq_rs"], tabs["q_cos"], tabs["q_sin"],
                   out_scale=out_scale, tm=MIXER_TM)


def _in_proj(x, p, ws, layer, mod, *, gates):
    h = _norm_modulate(x, p["norm1_g"], mod, shift_row=0, scale_row=1, tm=DENSE_TM)
    zs = _matmul(h, ws["w_small"], layer, sigmoid=False, out_dtype=F32, tm=DENSE_TM, tn=SMALL_TN, name="in_proj_small")
    g = None
    if gates:
        g = _matmul(h, ws["w_gate"], layer, sigmoid=True, out_dtype=BF16, tm=DENSE_TM, tn=GATE_TN,
                    name="in_proj_gate")
    return zs, g


def _mix_and_ffn(x, zs, gates, attn, p, w, ws, layer, mod):
    pool_out = _pool_mix(zs, w["pool_w"], p["pool_scale"], tm=MIXER_TM)
    sg_out = _spatial_gating(zs, p["sg_norm_g"], w["sg_w"], p["sg_b"], tm=MIXER_TM)
    x = _merge_out(pool_out, attn, sg_out, gates, ws["w_br_pool"], ws["w_br_mla"], ws["w_br_sg"], ws["w_o"], layer,
                   x, mod, tm=DENSE_TM, tn=MERGE_TN)
    h2 = _norm_modulate_tiles(x, p["norm2_g"], mod, shift_row=3, scale_row=4, tm=DENSE_TM)
    act = _ffn_up(h2, ws["ffn_up"], layer, p["ffn_conv_w"], p["ffn_conv_b"], tf=FFN_TF)
    return _mm_residual(act, ws["ffn_down"], layer, x, mod, 5, tm=DENSE_TM, tn=DOWN_TN, name="ffn_down_residual")


def kernel(x, c, ctx, c_ctx, ada_w, ada_b, norm1_g, w_in, pool_w, pool_scale, q_lat_g, w_uq, kv_lat_g, w_ukv, q_norm_g, k_norm_g, sg_norm_g, sg_w, sg_b, w_br_pool, w_br_mla, w_br_sg, w_o, norm2_g, ffn_up, ffn_conv_w, ffn_conv_b, ffn_down):
    b, l, d = x.shape
    lc = ctx.shape[1]
    depth = ada_w.shape[0]
    assert b + 1 <= SUBLANES

    cvec = jnp.concatenate([c, c_ctx[None, :], jnp.zeros((SUBLANES - b - 1, d), c.dtype)], axis=0)
    mod = _ada(cvec, ada_w, ada_b).reshape(depth, SUBLANES, 6, d)
    rope_lat = _rope_tables(l, rotate=True)
    rope_ctx = _rope_tables(lc, rotate=False)
    ws = _stacked_weights(w_in, w_br_pool, w_br_mla, w_br_sg, w_o, ffn_up, ffn_down)

    xl, xc = x, ctx
    for i in range(depth):
        p = {
            "norm1_g": norm1_g[i], "pool_w": pool_w[i], "pool_scale": pool_scale[i],
            "q_lat_g": q_lat_g[i], "w_uq": w_uq[i], "kv_lat_g": kv_lat_g[i], "w_ukv": w_ukv[i],
            "q_norm_g": q_norm_g[i], "k_norm_g": k_norm_g[i], "sg_norm_g": sg_norm_g[i], "sg_w": sg_w[i],
            "sg_b": sg_b[i], "norm2_g": norm2_g[i], "ffn_conv_w": ffn_conv_w[i], "ffn_conv_b": ffn_conv_b[i],
        }
        w = _layer_weights(p)
        mod_lat = mod[i, :b]
        mod_ctx = jnp.broadcast_to(mod[i, b:b + 1], (b, 6, d))

        zl, gl = _in_proj(xl, p, ws, i, mod_lat, gates=True)
        zc, gc = _in_proj(xc, p, ws, i, mod_ctx, gates=i < depth - 1)
        kc, vc = _keys_values(zc, p, w, rope_ctx)
        kl, vl = _keys_values(zl, p, w, rope_lat)
        ql = _queries(zl, p, w, rope_lat)
        attn_l = _attention(ql, [(kl, vl), (kc, vc)], bq=ATT_BQ, bk=ATT_BK)
        xl = _mix_and_ffn(xl, zl, gl, attn_l, p, w, ws, i, mod_lat)
        if i < depth - 1:
            qc = _queries(zc, p, w, rope_ctx)
            attn_c = _attention(qc, [(kc, vc)], bq=ATT_BQ, bk=ATT_BK)
            xc = _mix_and_ffn(xc, zc, gc, attn_c, p, w, ws, i, mod_ctx)
    return xl
```

```python
import functools
import math

import jax
import jax.numpy as jnp
import numpy as np
from jax import lax
from jax.experimental import pallas as pl
from jax.experimental.pallas import tpu as pltpu

F32 = jnp.float32
BF16 = jnp.bfloat16

GRID_W = 64
EPS = 1e-6
POOL_W = 512
POOL_WINDOWS = (2, 4, 8, 16)
POOL_GROUP_W = POOL_W // len(POOL_WINDOWS)
N_HEADS = 8
Q_LORA = 512
KV_LORA = 256
QK_NOPE = 128
QK_ROPE = 64
QK_HEAD = QK_NOPE + QK_ROPE
V_HEAD = 128
ROPE_FREQS = QK_ROPE // 4
ROPE_THETA = 10000.0
SG_W = 512
SG_GROUPS = 4
SG_GROUP_W = SG_W // SG_GROUPS
CHUNK = 128
CONV_W = 3

LANES = 128
SUBLANES = 8
BF16_ROWS = 16
HEAD_PAD = 2 * LANES

VMEM_LIMIT_BYTES = 56 * 1024 * 1024
ADA_TN = 1024
DENSE_TM = 1024
MIXER_TM = 1024
SMALL_TN = 1280
GATE_TN = 1024
MERGE_TN = 512
FFN_TF = 512
DOWN_TN = 512
ATT_BQ = 2048
ATT_BK = 512
ATT_UNROLL = 32

ZS_POOL = 0
ZS_Q = 512
ZS_SG = 1024
ZS_KV = 2048
ZS_COLS = 2560
ROPE_SWAP = np.concatenate([np.arange(16, 32), np.arange(0, 16), np.arange(48, 64), np.arange(32, 48)])


def _cparams(*sem):
    return pltpu.CompilerParams(dimension_semantics=sem, vmem_limit_bytes=VMEM_LIMIT_BYTES)


def _rms_scale(x, width):
    return lax.rsqrt(jnp.sum(x * x, axis=-1, keepdims=True) * (1.0 / width) + EPS)


def _ada_kernel(c_ref, w_ref, b_ref, o_ref):
    c = c_ref[...]
    s = c * jax.nn.sigmoid(c)
    o_ref[...] = jnp.dot(s, w_ref[...], preferred_element_type=F32, precision=lax.Precision.HIGHEST) + b_ref[...]


def _ada(cvec, ada_w, ada_b):
    depth, d, n = ada_w.shape
    tn = ADA_TN
    return pl.pallas_call(
        _ada_kernel,
        grid=(depth, n // tn),
        in_specs=[
            pl.BlockSpec((SUBLANES, d), lambda l, j: (0, 0)),
            pl.BlockSpec((None, d, tn), lambda l, j: (l, 0, j)),
            pl.BlockSpec((None, 1, tn), lambda l, j: (l, 0, j)),
        ],
        out_specs=pl.BlockSpec((None, SUBLANES, tn), lambda l, j: (l, 0, j)),
        out_shape=jax.ShapeDtypeStruct((depth, SUBLANES, n), F32),
        compiler_params=_cparams("parallel", "arbitrary"),
        name="ada_modulation",
    )(cvec, ada_w, ada_b.reshape(depth, 1, n))


def _norm_mod(x, g, mod_ref, shift_row, scale_row):
    y = x * _rms_scale(x, x.shape[-1]) * g
    return y * (1.0 + mod_ref[scale_row:scale_row + 1, :]) + mod_ref[shift_row:shift_row + 1, :]


def _sigmoid(x):
    return 0.5 * jnp.tanh(0.5 * x) + 0.5


def _norm_kernel(x_ref, g_ref, mod_ref, o_ref, *, shift_row, scale_row):
    o_ref[...] = _norm_mod(x_ref[...], g_ref[...], mod_ref, shift_row, scale_row).astype(o_ref.dtype)


def _norm_modulate(x, g, mod, *, shift_row, scale_row, tm):
    b, l, d = x.shape
    tm = min(tm, l)
    return pl.pallas_call(
        functools.partial(_norm_kernel, shift_row=shift_row, scale_row=scale_row),
        grid=(b, l // tm),
        in_specs=[
            pl.BlockSpec((None, tm, d), lambda bi, i: (bi, i, 0)),
            pl.BlockSpec((1, d), lambda bi, i: (0, 0)),
            pl.BlockSpec((None, 6, d), lambda bi, i: (bi, 0, 0)),
        ],
        out_specs=pl.BlockSpec((None, tm, d), lambda bi, i: (bi, i, 0)),
        out_shape=jax.ShapeDtypeStruct((b, l, d), BF16),
        compiler_params=_cparams("parallel", "parallel"),
        name="norm_modulate",
    )(x, g.reshape(1, d), mod)


def _norm_halo_kernel(xp_ref, xc_ref, xn_ref, g_ref, mod_ref, o_ref, *, tm, nt, shift_row, scale_row):
    ti = pl.program_id(1)
    halo = BF16_ROWS
    g = g_ref[...]
    o_ref[halo:halo + tm, :] = _norm_mod(xc_ref[...], g, mod_ref, shift_row, scale_row).astype(o_ref.dtype)
    prev = jnp.where(ti > 0, _norm_mod(xp_ref[...], g, mod_ref, shift_row, scale_row), 0.0)
    o_ref[0:halo, :] = prev.astype(o_ref.dtype)
    nxt = jnp.where(ti < nt - 1, _norm_mod(xn_ref[...], g, mod_ref, shift_row, scale_row), 0.0)
    o_ref[halo + tm:, :] = nxt.astype(o_ref.dtype)


def _norm_modulate_tiles(x, g, mod, *, shift_row, scale_row, tm):
    b, l, d = x.shape
    tm = min(tm, l)
    nt = l // tm
    hb = tm // BF16_ROWS
    rows = tm + 2 * BF16_ROWS
    return pl.pallas_call(
        functools.partial(_norm_halo_kernel, tm=tm, nt=nt, shift_row=shift_row, scale_row=scale_row),
        grid=(b, nt),
        in_specs=[
            pl.BlockSpec((None, BF16_ROWS, d), lambda bi, i: (bi, jnp.maximum(i * hb - 1, 0), 0)),
            pl.BlockSpec((None, tm, d), lambda bi, i: (bi, i, 0)),
            pl.BlockSpec((None, BF16_ROWS, d), lambda bi, i: (bi, jnp.minimum((i + 1) * hb, l // BF16_ROWS - 1), 0)),
            pl.BlockSpec((1, d), lambda bi, i: (0, 0)),
            pl.BlockSpec((None, 6, d), lambda bi, i: (bi, 0, 0)),
        ],
        out_specs=pl.BlockSpec((None, None, rows, d), lambda bi, i: (bi, i, 0, 0)),
        out_shape=jax.ShapeDtypeStruct((b, nt, rows, d), BF16),
        compiler_params=_cparams("parallel", "parallel"),
        name="norm_modulate_tiles",
    )(x, x, x, g.reshape(1, d), mod)


def _matmul_kernel(a_ref, w_ref, o_ref, *, sigmoid):
    z = jnp.dot(a_ref[...], w_ref[...], preferred_element_type=F32)
    if sigmoid:
        z = _sigmoid(z)
    o_ref[...] = z.astype(o_ref.dtype)


def _matmul(a, w, layer, *, sigmoid, out_dtype, tm, tn, name):
    b, l, k = a.shape
    n = w.shape[2]
    tm = min(tm, l)
    return pl.pallas_call(
        functools.partial(_matmul_kernel, sigmoid=sigmoid),
        grid=(b, l // tm, n // tn),
        in_specs=[
            pl.BlockSpec((None, tm, k), lambda bi, i, j: (bi, i, 0)),
            pl.BlockSpec((None, k, tn), lambda bi, i, j: (layer, 0, j)),
        ],
        out_specs=pl.BlockSpec((None, tm, tn), lambda bi, i, j: (bi, i, j)),
        out_shape=jax.ShapeDtypeStruct((b, l, n), out_dtype),
        compiler_params=_cparams("parallel", "parallel", "arbitrary"),
        name=name,
    )(a, w)


def _pool_kernel(prev_ref, cur_ref, next_ref, pw_ref, ps_ref, o_ref, ext_ref, *, tm, nt, seq_len):
    ti = pl.program_id(1)
    cur = cur_ref[...]
    ext_ref[SUBLANES:SUBLANES + tm, :] = cur
    ext_ref[0:SUBLANES, :] = jnp.where(ti > 0, prev_ref[...], 0.0)
    ext_ref[SUBLANES + tm:, :] = jnp.where(ti < nt - 1, next_ref[...], 0.0)
    t = ti * tm + lax.broadcasted_iota(jnp.int32, (tm, 1), 0)
    for g, w in enumerate(POOL_WINDOWS):
        r = w // 2
        c0 = g * POOL_GROUP_W
        s = ext_ref[SUBLANES - r:SUBLANES - r + tm, c0:c0 + POOL_GROUP_W]
        for dlt in range(-r + 1, r):
            s = s + ext_ref[SUBLANES + dlt:SUBLANES + dlt + tm, c0:c0 + POOL_GROUP_W]
        cnt = (jnp.minimum(t + r, seq_len) - jnp.maximum(t - r, 0)).astype(F32)
        pooled = s / cnt - cur[:, c0:c0 + POOL_GROUP_W]
        y = jnp.dot(pooled.astype(BF16), pw_ref[g], preferred_element_type=F32)
        o_ref[:, c0:c0 + POOL_GROUP_W] = (y * ps_ref[:, c0:c0 + POOL_GROUP_W]).astype(o_ref.dtype)


def _pool_mix(zs, pool_w, pool_scale, *, tm):
    b, l, _ = zs.shape
    tm = min(tm, l)
    nt = l // tm
    hb = tm // SUBLANES
    col = ZS_POOL // POOL_W
    return pl.pallas_call(
        functools.partial(_pool_kernel, tm=tm, nt=nt, seq_len=l),
        grid=(b, nt),
        in_specs=[
            pl.BlockSpec((None, SUBLANES, POOL_W), lambda bi, i: (bi, jnp.maximum(i * hb - 1, 0), col)),
            pl.BlockSpec((None, tm, POOL_W), lambda bi, i: (bi, i, col)),
            pl.BlockSpec((None, SUBLANES, POOL_W),
                         lambda bi, i: (bi, jnp.minimum((i + 1) * hb, l // SUBLANES - 1), col)),
            pl.BlockSpec((len(POOL_WINDOWS), POOL_GROUP_W, POOL_GROUP_W), lambda bi, i: (0, 0, 0)),
            pl.BlockSpec((1, POOL_W), lambda bi, i: (0, 0)),
        ],
        out_specs=pl.BlockSpec((None, tm, POOL_W), lambda bi, i: (bi, i, 0)),
        out_shape=jax.ShapeDtypeStruct((b, l, POOL_W), BF16),
        scratch_shapes=[pltpu.VMEM((tm + 2 * SUBLANES, POOL_W), F32)],
        compiler_params=_cparams("parallel", "parallel"),
        name="pool_mix",
    )(zs, zs, zs, pool_w, pool_scale.reshape(1, POOL_W))


def _sg_kernel(z_ref, g_ref, w_ref, b_ref, o_ref, *, tm):
    z = z_ref[...]
    a = z * (0.5 * (1.0 + jnp.tanh(math.sqrt(2.0 / math.pi) * (z + 0.044715 * (z * z * z)))))
    u = a[:, :SG_W]
    v = a[:, SG_W:]
    vc = v - jnp.mean(v, axis=-1, keepdims=True)
    vn = (vc * lax.rsqrt(jnp.mean(vc * vc, axis=-1, keepdims=True) + EPS) * g_ref[...]).astype(BF16)
    for c in range(tm // CHUNK):
        rows = slice(c * CHUNK, (c + 1) * CHUNK)
        for g in range(SG_GROUPS):
            cols = slice(g * SG_GROUP_W, (g + 1) * SG_GROUP_W)
            s = jnp.dot(w_ref[g], vn[rows, cols], preferred_element_type=F32) + b_ref[g]
            o_ref[rows, cols] = (u[rows, cols] * s).astype(o_ref.dtype)


def _spatial_gating(zs, sg_norm_g, sg_w, sg_b, *, tm):
    b, l, _ = zs.shape
    tm = min(tm, l)
    return pl.pallas_call(
        functools.partial(_sg_kernel, tm=tm),
        grid=(b, l // tm),
        in_specs=[
            pl.BlockSpec((None, tm, 2 * SG_W), lambda bi, i: (bi, i, ZS_SG // (2 * SG_W))),
            pl.BlockSpec((1, SG_W), lambda bi, i: (0, 0)),
            pl.BlockSpec((SG_GROUPS, CHUNK, CHUNK), lambda bi, i: (0, 0, 0)),
            pl.BlockSpec((SG_GROUPS, CHUNK, 1), lambda bi, i: (0, 0, 0)),
        ],
        out_specs=pl.BlockSpec((None, tm, SG_W), lambda bi, i: (bi, i, 0)),
        out_shape=jax.ShapeDtypeStruct((b, l, SG_W), BF16),
        compiler_params=_cparams("parallel", "parallel"),
        name="spatial_gating",
    )(zs, sg_norm_g.reshape(1, SG_W), sg_w, sg_b.reshape(SG_GROUPS, CHUNK, 1))


def _q_kernel(z_ref, gl_ref, w_ref, gn_ref, gr_ref, grs_ref, cos_ref, sin_ref, o_ref, *, out_scale):
    z = z_ref[...]
    zn = (z * _rms_scale(z, Q_LORA) * gl_ref[...]).astype(BF16)
    q = jnp.dot(zn, w_ref[...], preferred_element_type=F32)
    nope_w = N_HEADS * QK_NOPE
    rope_w = N_HEADS * QK_ROPE
    lo = lax.broadcasted_iota(jnp.int32, (1, LANES), 1) < QK_ROPE
    cos = cos_ref[...]
    sin = sin_ref[...]
    gn = gn_ref[...] * out_scale
    for pair in range(N_HEADS // 2):
        blk = slice(pair * LANES, (pair + 1) * LANES)
        r2 = q[:, nope_w + pair * LANES:nope_w + (pair + 1) * LANES]
        r2s = q[:, nope_w + rope_w + pair * LANES:nope_w + rope_w + (pair + 1) * LANES]
        sq = r2 * r2
        ss_lo = jnp.sum(jnp.where(lo, sq, 0.0), axis=-1, keepdims=True)
        ss_hi = jnp.sum(jnp.where(lo, 0.0, sq), axis=-1, keepdims=True)
        rot = r2 * gr_ref[:, blk] * cos + r2s * grs_ref[:, blk] * sin
        for half, ss_r in enumerate((ss_lo, ss_hi)):
            h = 2 * pair + half
            nope = q[:, h * QK_NOPE:(h + 1) * QK_NOPE]
            inv = lax.rsqrt((jnp.sum(nope * nope, axis=-1, keepdims=True) + ss_r) * (1.0 / QK_HEAD) + EPS)
            o_ref[h, :, 0:LANES] = (nope * inv * gn).astype(o_ref.dtype)
            keep = lo if half == 0 else jnp.logical_not(lo)
            o_ref[h, :, LANES:HEAD_PAD] = jnp.where(keep, rot * (inv * out_scale), 0.0).astype(o_ref.dtype)


def _q_proj(zs, q_lat_g, w_q, gn, gr, grs, cos_tab, sin_tab, *, out_scale, tm):
    b, l, _ = zs.shape
    tm = min(tm, l)
    return pl.pallas_call(
        functools.partial(_q_kernel, out_scale=out_scale),
        grid=(b, l // tm),
        in_specs=[
            pl.BlockSpec((None, tm, Q_LORA), lambda bi, i: (bi, i, ZS_Q // Q_LORA)),
            pl.BlockSpec((1, Q_LORA), lambda bi, i: (0, 0)),
            pl.BlockSpec(w_q.shape, lambda bi, i: (0, 0)),
            pl.BlockSpec((1, QK_NOPE), lambda bi, i: (0, 0)),
            pl.BlockSpec((1, N_HEADS * QK_ROPE), lambda bi, i: (0, 0)),
            pl.BlockSpec((1, N_HEADS * QK_ROPE), lambda bi, i: (0, 0)),
            pl.BlockSpec((tm, LANES), lambda bi, i: (i, 0)),
            pl.BlockSpec((tm, LANES), lambda bi, i: (i, 0)),
        ],
        out_specs=pl.BlockSpec((None, N_HEADS, tm, HEAD_PAD), lambda bi, i: (bi, 0, i, 0)),
        out_shape=jax.ShapeDtypeStruct((b, N_HEADS, l, HEAD_PAD), BF16),
        compiler_params=_cparams("parallel", "parallel"),
        name="q_proj",
    )(zs, q_lat_g.reshape(1, Q_LORA), w_q, gn, gr, grs, cos_tab, sin_tab)


def _kv_kernel(z_ref, gl_ref, w_ref, gn_ref, gt_ref, tab_ref, k_ref, v_ref):
    z = z_ref[...]
    zkv = z[:, :KV_LORA]
    zn = (zkv * _rms_scale(zkv, KV_LORA) * gl_ref[...]).astype(BF16)
    kv = jnp.dot(zn, w_ref[...], preferred_element_type=F32)
    lo = lax.broadcasted_iota(jnp.int32, (1, LANES), 1) < QK_ROPE
    kr2 = z[:, KV_LORA:KV_LORA + LANES]
    ss_r = jnp.sum(jnp.where(lo, kr2 * kr2, 0.0), axis=-1, keepdims=True)
    t = kr2 * gt_ref[...] * tab_ref[...]
    rot = t + pltpu.roll(t, QK_ROPE, axis=1)
    gn = gn_ref[...]
    ones = jnp.ones((z.shape[0], LANES), v_ref.dtype)
    for h in range(N_HEADS):
        nope = kv[:, h * QK_NOPE:(h + 1) * QK_NOPE]
        inv = lax.rsqrt((jnp.sum(nope * nope, axis=-1, keepdims=True) + ss_r) * (1.0 / QK_HEAD) + EPS)
        k_ref[h, :, 0:LANES] = (nope * inv * gn).astype(k_ref.dtype)
        keep = lo if h % 2 == 0 else jnp.logical_not(lo)
        k_ref[h, :, LANES:HEAD_PAD] = jnp.where(keep, rot * inv, 0.0).astype(k_ref.dtype)
        v_ref[h, :, 0:LANES] = kv[:, N_HEADS * QK_NOPE + h * V_HEAD:N_HEADS * QK_NOPE + (h + 1) * V_HEAD].astype(
            v_ref.dtype)
        v_ref[h, :, LANES:HEAD_PAD] = ones


def _kv_proj(zs, kv_lat_g, w_kv, gn, gt, tab, *, tm):
    b, l, _ = zs.shape
    tm = min(tm, l)
    blk = KV_LORA + 2 * LANES
    shape = jax.ShapeDtypeStruct((b, N_HEADS, l, HEAD_PAD), BF16)
    spec = pl.BlockSpec((None, N_HEADS, tm, HEAD_PAD), lambda bi, i: (bi, 0, i, 0))
    return pl.pallas_call(
        _kv_kernel,
        grid=(b, l // tm),
        in_specs=[
            pl.BlockSpec((None, tm, blk), lambda bi, i: (bi, i, ZS_KV // blk)),
            pl.BlockSpec((1, KV_LORA), lambda bi, i: (0, 0)),
            pl.BlockSpec(w_kv.shape, lambda bi, i: (0, 0)),
            pl.BlockSpec((1, QK_NOPE), lambda bi, i: (0, 0)),
            pl.BlockSpec((1, LANES), lambda bi, i: (0, 0)),
            pl.BlockSpec((tm, LANES), lambda bi, i: (i, 0)),
        ],
        out_specs=[spec, spec],
        out_shape=[shape, shape],
        compiler_params=_cparams("parallel", "parallel"),
        name="kv_proj",
    )(zs, kv_lat_g.reshape(1, KV_LORA), w_kv, gn, gt, tab)


def _attn_kernel(q_ref, *refs, chunks):
    n_src = len(chunks)
    o_ref, acc_ref, m_ref = refs[2 * n_src:]
    q = q_ref[...]
    acc_ref[...] = jnp.zeros_like(acc_ref)
    m_ref[...] = jnp.full_like(m_ref, -jnp.inf)

    def step(k, v):
        s = lax.dot_general(q, k, (((1,), (1,)), ((), ())), preferred_element_type=F32)
        m_prev = m_ref[...]
        m_new = jnp.maximum(m_prev, jnp.max(s, axis=-1, keepdims=True))
        alpha = jnp.exp2(m_prev - m_new)
        p = jnp.exp2(s - jnp.concatenate([m_new] * (s.shape[1] // LANES), axis=1)).astype(BF16)
        pv = jnp.dot(p, v, preferred_element_type=F32)
        acc_ref[...] = acc_ref[...] * jnp.concatenate([alpha, alpha], axis=1) + pv
        m_ref[...] = m_new

    for si, (bk, n) in enumerate(chunks):
        k_ref, v_ref = refs[2 * si], refs[2 * si + 1]
        if n == 1:
            step(k_ref[...], v_ref[...])
        else:
            def body(c, carry, k_ref=k_ref, v_ref=v_ref, bk=bk):
                off = pl.multiple_of(c * bk, bk)
                step(k_ref[pl.ds(off, bk), :], v_ref[pl.ds(off, bk), :])
                return carry

            lax.fori_loop(0, n, body, 0, unroll=ATT_UNROLL)
    acc = acc_ref[...]
    o_ref[...] = (acc[:, :V_HEAD] / acc[:, V_HEAD:]).astype(o_ref.dtype)


def _attention(q, sources, *, bq, bk):
    b, h, l, _ = q.shape
    bq = min(bq, l)
    chunks = []
    in_specs = [pl.BlockSpec((None, None, bq, HEAD_PAD), lambda bi, hi, i: (bi, hi, i, 0))]
    args = [q]
    for k, v in sources:
        lk = k.shape[2]
        cb = min(bk, lk)
        chunks.append((cb, lk // cb))
        for a in (k, v):
            in_specs.append(pl.BlockSpec((None, None, lk, HEAD_PAD), lambda bi, hi, i: (bi, hi, 0, 0)))
            args.append(a)
    return pl.pallas_call(
        functools.partial(_attn_kernel, chunks=tuple(chunks)),
        grid=(b, h, l // bq),
        in_specs=in_specs,
        out_specs=pl.BlockSpec((None, bq, V_HEAD), lambda bi, hi, i: (bi, i, hi)),
        out_shape=jax.ShapeDtypeStruct((b, l, h * V_HEAD), BF16),
        scratch_shapes=[pltpu.VMEM((bq, HEAD_PAD), F32), pltpu.VMEM((bq, LANES), F32)],
        compiler_params=_cparams("parallel", "parallel", "arbitrary"),
        name="latent_attention",
    )(*args)


def _merge_out_kernel(p_ref, a_ref, s_ref, gp_ref, ga_ref, gs_ref, wp_ref, wa_ref, ws_ref, wo_ref, x_ref, g1_ref,
                      o_ref, y_ref, *, nj, tn):
    j = pl.program_id(2)

    @pl.when(j < nj)
    def _():
        y = gp_ref[...].astype(F32) * jnp.dot(p_ref[...], wp_ref[j], preferred_element_type=F32)
        y += ga_ref[...].astype(F32) * jnp.dot(a_ref[...], wa_ref[j], preferred_element_type=F32)
        y += gs_ref[...].astype(F32) * jnp.dot(s_ref[...], ws_ref[j], preferred_element_type=F32)
        y_ref[j] = y.astype(y_ref.dtype)

    @pl.when(j >= nj)
    def _():
        wo = wo_ref.at[j - nj]
        acc = jnp.dot(y_ref[0], wo[0:tn, :], preferred_element_type=F32)
        for c in range(1, nj):
            acc += jnp.dot(y_ref[c], wo[c * tn:(c + 1) * tn, :], preferred_element_type=F32)
        o_ref[...] = x_ref[...] + g1_ref[...] * acc


def _merge_out(pool_out, attn, sg_out, gates, w_p, w_a, w_s, w_o, layer, x, mod, *, tm, tn):
    b, l, _ = pool_out.shape
    nj = w_p.shape[1]
    d = nj * tn
    tm = min(tm, l)

    def first(j):
        return jnp.minimum(j, nj - 1)

    def second(j):
        return jnp.maximum(j - nj, 0)

    def act(width):
        return pl.BlockSpec((None, tm, width), lambda bi, i, j: (bi, i, 0))

    def gate(branch):
        return pl.BlockSpec((None, tm, tn), lambda bi, i, j: (bi, i, branch * nj + first(j)))

    def wgt(width):
        return pl.BlockSpec((None, nj, width, tn), lambda bi, i, j: (layer, 0, 0, 0), pipeline_mode=pl.Buffered(1))

    return pl.pallas_call(
        functools.partial(_merge_out_kernel, nj=nj, tn=tn),
        grid=(b, l // tm, 2 * nj),
        in_specs=[act(POOL_W), act(N_HEADS * V_HEAD), act(SG_W), gate(0), gate(1), gate(2),
                  wgt(POOL_W), wgt(N_HEADS * V_HEAD), wgt(SG_W), wgt(d),
                  pl.BlockSpec((None, tm, tn), lambda bi, i, j: (bi, i, second(j))),
                  pl.BlockSpec((None, 1, tn), lambda bi, i, j: (bi, 0, second(j)))],
        out_specs=pl.BlockSpec((None, tm, tn), lambda bi, i, j: (bi, i, second(j))),
        out_shape=jax.ShapeDtypeStruct((b, l, d), F32),
        scratch_shapes=[pltpu.VMEM((nj, tm, tn), BF16)],
        compiler_params=_cparams("parallel", "parallel", "arbitrary"),
        name="merge_out_proj",
    )(pool_out, attn, sg_out, gates, gates, gates, w_p, w_a, w_s, w_o, x, mod[:, 2:3, :])


def _mm_res_kernel(a_ref, w_ref, x_ref, gate_ref, o_ref):
    y = jnp.dot(a_ref[...], w_ref[...], preferred_element_type=F32)
    o_ref[...] = x_ref[...] + gate_ref[...] * y


def _mm_residual(a, w, layer, x, mod, gate_row, *, tm, tn, name):
    b, l, k = a.shape
    d = w.shape[2]
    tm = min(tm, l)
    return pl.pallas_call(
        _mm_res_kernel,
        grid=(b, l // tm, d // tn),
        in_specs=[
            pl.BlockSpec((None, tm, k), lambda bi, i, j: (bi, i, 0)),
            pl.BlockSpec((None, k, tn), lambda bi, i, j: (layer, 0, j)),
            pl.BlockSpec((None, tm, tn), lambda bi, i, j: (bi, i, j)),
            pl.BlockSpec((None, 1, tn), lambda bi, i, j: (bi, 0, j)),
        ],
        out_specs=pl.BlockSpec((None, tm, tn), lambda bi, i, j: (bi, i, j)),
        out_shape=jax.ShapeDtypeStruct((b, l, d), F32),
        compiler_params=_cparams("parallel", "parallel", "arbitrary"),
        name=name,
    )(a, w, x, mod[:, gate_row:gate_row + 1, :])


def _ffn_up_kernel(h_ref, wa_ref, wv_ref, cwa_ref, cwv_ref, cba_ref, cbv_ref, o_ref, *, tm):
    halo = BF16_ROWS
    h = h_ref[...]
    rows = h.shape[0]

    def conv(w_ref, cw_ref, cb_ref):
        u = jnp.dot(h, w_ref[...], preferred_element_type=F32)
        acc = cb_ref[...] + pltpu.roll(u, 1, axis=0) * cw_ref[0:1, :]
        acc = acc + u * cw_ref[1:2, :]
        acc = acc + pltpu.roll(u, rows - 1, axis=0) * cw_ref[2:3, :]
        return acc[halo:halo + tm, :]

    a = conv(wa_ref, cwa_ref, cba_ref)
    v = conv(wv_ref, cwv_ref, cbv_ref)
    o_ref[...] = (a * _sigmoid(a) * v).astype(o_ref.dtype)


def _ffn_up(h_tiles, w_up, layer, conv_w, conv_b, *, tf):
    b, nt, rows, d = h_tiles.shape
    tm = rows - 2 * BF16_ROWS
    l = nt * tm
    d_ff = w_up.shape[2] // 2
    nf = d_ff // tf
    return pl.pallas_call(
        functools.partial(_ffn_up_kernel, tm=tm),
        grid=(b, nt, nf),
        in_specs=[
            pl.BlockSpec((None, None, rows, d), lambda bi, i, j: (bi, i, 0, 0)),
            pl.BlockSpec((None, d, tf), lambda bi, i, j: (layer, 0, j)),
            pl.BlockSpec((None, d, tf), lambda bi, i, j: (layer, 0, nf + j)),
            pl.BlockSpec((CONV_W, tf), lambda bi, i, j: (0, j)),
            pl.BlockSpec((CONV_W, tf), lambda bi, i, j: (0, nf + j)),
            pl.BlockSpec((1, tf), lambda bi, i, j: (0, j)),
            pl.BlockSpec((1, tf), lambda bi, i, j: (0, nf + j)),
        ],
        out_specs=pl.BlockSpec((None, tm, tf), lambda bi, i, j: (bi, i, j)),
        out_shape=jax.ShapeDtypeStruct((b, l, d_ff), BF16),
        compiler_params=_cparams("parallel", "parallel", "arbitrary"),
        name="ffn_up_conv_gate",
    )(h_tiles, w_up, w_up, conv_w, conv_w, conv_b.reshape(1, -1), conv_b.reshape(1, -1))


def _rope_tables(n_tokens, rotate):
    if rotate:
        t = np.arange(n_tokens)
        inv_freq = ROPE_THETA ** (-np.arange(ROPE_FREQS, dtype=np.float64) / ROPE_FREQS)
        ar = (t // GRID_W)[:, None] * inv_freq
        ac = (t % GRID_W)[:, None] * inv_freq
        c = np.concatenate([np.cos(ar), np.cos(ar), np.cos(ac), np.cos(ac)], axis=1)
        s = np.concatenate([-np.sin(ar), np.sin(ar), -np.sin(ac), np.sin(ac)], axis=1)
    else:
        c, s = np.ones((n_tokens, QK_ROPE)), np.zeros((n_tokens, QK_ROPE))
    cat = lambda a, b_: jnp.asarray(np.concatenate([a, b_], axis=1), F32)
    return dict(q_cos=cat(c, c), q_sin=cat(s, s), k=cat(c, s))


def _stacked_weights(w_in, w_br_pool, w_br_mla, w_br_sg, w_o, ffn_up, ffn_down):
    depth, d, _ = w_in.shape
    o_q, o_kv, o_kr, o_sg, o_gate = POOL_W, POOL_W + Q_LORA, POOL_W + Q_LORA + KV_LORA, \
        POOL_W + Q_LORA + KV_LORA + QK_ROPE, POOL_W + Q_LORA + KV_LORA + QK_ROPE + 2 * SG_W
    w_kr = w_in[:, :, o_kr:o_sg]
    w_small = jnp.concatenate([
        w_in[:, :, :o_q], w_in[:, :, o_q:o_kv], w_in[:, :, o_sg:o_gate], w_in[:, :, o_kv:o_kr], w_kr,
        w_kr[:, :, ROPE_SWAP], jnp.zeros((depth, d, ZS_COLS - ZS_KV - KV_LORA - 2 * QK_ROPE), w_in.dtype)],
        axis=2).astype(BF16)
    def col_tiles(w):
        dep, k, n = w.shape
        return w.astype(BF16).reshape(dep, k, n // MERGE_TN, MERGE_TN).transpose(0, 2, 1, 3)

    return dict(
        w_small=w_small, w_gate=w_in[:, :, o_gate:].astype(BF16), w_br_pool=col_tiles(w_br_pool),
        w_br_mla=col_tiles(w_br_mla), w_br_sg=col_tiles(w_br_sg), w_o=col_tiles(w_o),
        ffn_up=ffn_up.astype(BF16), ffn_down=ffn_down.astype(BF16))


def _layer_weights(p):
    w_uq = p["w_uq"].reshape(Q_LORA, N_HEADS, QK_HEAD)
    w_q_rope = w_uq[:, :, QK_NOPE:]
    w_q = jnp.concatenate([
        w_uq[:, :, :QK_NOPE].reshape(Q_LORA, -1), w_q_rope.reshape(Q_LORA, -1),
        w_q_rope[:, :, ROPE_SWAP].reshape(Q_LORA, -1)], axis=1).astype(BF16)
    gq = p["q_norm_g"]
    gq_n = gq[:QK_NOPE].reshape(1, QK_NOPE)
    gq_r = jnp.tile(gq[QK_NOPE:], N_HEADS).reshape(1, -1)
    gq_rs = jnp.tile(gq[QK_NOPE:][ROPE_SWAP], N_HEADS).reshape(1, -1)

    w_ukv = p["w_ukv"].reshape(KV_LORA, N_HEADS, QK_NOPE + V_HEAD)
    w_kv = jnp.concatenate([
        w_ukv[:, :, :QK_NOPE].reshape(KV_LORA, -1), w_ukv[:, :, QK_NOPE:].reshape(KV_LORA, -1)], axis=1).astype(BF16)
    gk = p["k_norm_g"]
    gk_n = gk[:QK_NOPE].reshape(1, QK_NOPE)
    gk_t = jnp.concatenate([gk[QK_NOPE:], gk[QK_NOPE:][ROPE_SWAP]]).reshape(1, LANES)
    return dict(
        w_q=w_q, gq_n=gq_n, gq_r=gq_r, gq_rs=gq_rs, w_kv=w_kv, gk_n=gk_n, gk_t=gk_t,
        pool_w=p["pool_w"].astype(BF16), sg_w=p["sg_w"].astype(BF16))


def _keys_values(zs, p, w, tabs):
    return _kv_proj(zs, p["kv_lat_g"], w["w_kv"], w["gk_n"], w["gk_t"], tabs["k"], tm=MIXER_TM)


def _queries(zs, p, w, tabs):
    out_scale = QK_HEAD ** -0.5 * math.log2(math.e)
    return _q_proj(zs, p["q_lat_g"], w["w_q"], w["gq_n"], w["gq_r"], w["gq_rs"], tabs["q_cos"], tabs["q_sin"],
                   out_scale=out_scale, tm=MIXER_TM)


def _in_proj(x, p, ws, layer, mod, *, gates):
    h = _norm_modulate(x, p["norm1_g"], mod, shift_row=0, scale_row=1, tm=DENSE_TM)
    zs = _matmul(h, ws["w_small"], layer, sigmoid=False, out_dtype=F32, tm=DENSE_TM, tn=SMALL_TN, name="in_proj_small")
    g = None
    if gates:
        g = _matmul(h, ws["w_gate"], layer, sigmoid=True, out_dtype=BF16, tm=DENSE_TM, tn=GATE_TN,
                    name="in_proj_gate")
    return zs, g


def _mix_and_ffn(x, zs, gates, attn, p, w, ws, layer, mod):
    pool_out = _pool_mix(zs, w["pool_w"], p["pool_scale"], tm=MIXER_TM)
    sg_out = _spatial_gating(zs, p["sg_norm_g"], w["sg_w"], p["sg_b"], tm=MIXER_TM)
    x = _merge_out(pool_out, attn, sg_out, gates, ws["w_br_pool"], ws["w_br_mla"], ws["w_br_sg"], ws["w_o"], layer,
                   x, mod, tm=DENSE_TM, tn=MERGE_TN)
    h2 = _norm_modulate_tiles(x, p["norm2_g"], mod, shift_row=3, scale_row=4, tm=DENSE_TM)
    act = _ffn_up(h2, ws["ffn_up"], layer, p["ffn_conv_w"], p["ffn_conv_b"], tf=FFN_TF)
    return _mm_residual(act, ws["ffn_down"], layer, x, mod, 5, tm=DENSE_TM, tn=DOWN_TN, name="ffn_down_residual")


def kernel(x, c, ctx, c_ctx, ada_w, ada_b, norm1_g, w_in, pool_w, pool_scale, q_lat_g, w_uq, kv_lat_g, w_ukv, q_norm_g, k_norm_g, sg_norm_g, sg_w, sg_b, w_br_pool, w_br_mla, w_br_sg, w_o, norm2_g, ffn_up, ffn_conv_w, ffn_conv_b, ffn_down):
    b, l, d = x.shape
    lc = ctx.shape[1]
    depth = ada_w.shape[0]
    assert b + 1 <= SUBLANES

    cvec = jnp.concatenate([c, c_ctx[None, :], jnp.zeros((SUBLANES - b - 1, d), c.dtype)], axis=0)
    mod = _ada(cvec, ada_w, ada_b).reshape(depth, SUBLANES, 6, d)
    rope_lat = _rope_tables(l, rotate=True)
    rope_ctx = _rope_tables(lc, rotate=False)
    ws = _stacked_weights(w_in, w_br_pool, w_br_mla, w_br_sg, w_o, ffn_up, ffn_down)

    xl, xc = x, ctx
    for i in range(depth):
        p = {
            "norm1_g": norm1_g[i], "pool_w": pool_w[i], "pool_scale": pool_scale[i],
            "q_lat_g": q_lat_g[i], "w_uq": w_uq[i], "kv_lat_g": kv_lat_g[i], "w_ukv": w_ukv[i],
            "q_norm_g": q_norm_g[i], "k_norm_g": k_norm_g[i], "sg_norm_g": sg_norm_g[i], "sg_w": sg_w[i],
            "sg_b": sg_b[i], "norm2_g": norm2_g[i], "ffn_conv_w": ffn_conv_w[i], "ffn_conv_b": ffn_conv_b[i],
        }
        w = _layer_weights(p)
        mod_lat = mod[i, :b]
        mod_ctx = jnp.broadcast_to(mod[i, b:b + 1], (b, 6, d))

        zl, gl = _in_proj(xl, p, ws, i, mod_lat, gates=True)
        zc, gc = _in_proj(xc, p, ws, i, mod_ctx, gates=i < depth - 1)
        kc, vc = _keys_values(zc, p, w, rope_ctx)
        kl, vl = _keys_values(zl, p, w, rope_lat)
        ql = _queries(zl, p, w, rope_lat)
        attn_l = _attention(ql, [(kl, vl), (kc, vc)], bq=ATT_BQ, bk=ATT_BK)
        xl = _mix_and_ffn(xl, zl, gl, attn_l, p, w, ws, i, mod_lat)
        if i < depth - 1:
            qc = _queries(zc, p, w, rope_ctx)
            attn_c = _attention(qc, [(kc, vc)], bq=ATT_BQ, bk=ATT_BK)
            xc = _mix_and_ffn(xc, zc, gc, attn_c, p, w, ws, i, mod_ctx)
    return xl
```

```python
import functools
import math

import jax
import jax.numpy as jnp
import numpy as np
from jax import lax
from jax.experimental import pallas as pl
from jax.experimental.pallas import tpu as pltpu

F32 = jnp.float32
BF16 = jnp.bfloat16

GRID_W = 64
EPS = 1e-6
POOL_W = 512
POOL_WINDOWS = (2, 4, 8, 16)
POOL_GROUP_W = POOL_W // len(POOL_WINDOWS)
N_HEADS = 8
Q_LORA = 512
KV_LORA = 256
QK_NOPE = 128
QK_ROPE = 64
QK_HEAD = QK_NOPE + QK_ROPE
V_HEAD = 128
ROPE_FREQS = QK_ROPE // 4
ROPE_THETA = 10000.0
SG_W = 512
SG_GROUPS = 4
SG_GROUP_W = SG_W // SG_GROUPS
CHUNK = 128
CONV_W = 3

LANES = 128
SUBLANES = 8
BF16_ROWS = 16
HEAD_PAD = 2 * LANES

VMEM_LIMIT_BYTES = 56 * 1024 * 1024
ADA_TN = 1024
DENSE_TM = 1024
MIXER_TM = 1024
SMALL_TN = 1280
GATE_TN = 2048
MERGE_TN = 512
FFN_TF = 512
DOWN_TN = 512
ATT_BQ = 2048
ATT_BK = 512
ATT_UNROLL = 16

ZS_POOL = 0
ZS_Q = 512
ZS_SG = 1024
ZS_KV = 2048
ZS_COLS = 2560
ROPE_SWAP = np.concatenate([np.arange(16, 32), np.arange(0, 16), np.arange(48, 64), np.arange(32, 48)])


def _cparams(*sem):
    return pltpu.CompilerParams(dimension_semantics=sem, vmem_limit_bytes=VMEM_LIMIT_BYTES)


def _rms_scale(x, width):
    return lax.rsqrt(jnp.sum(x * x, axis=-1, keepdims=True) * (1.0 / width) + EPS)


def _ada_kernel(c_ref, w_ref, b_ref, o_ref):
    c = c_ref[...]
    s = c * jax.nn.sigmoid(c)
    o_ref[...] = jnp.dot(s, w_ref[...], preferred_element_type=F32, precision=lax.Precision.HIGHEST) + b_ref[...]


def _ada(cvec, ada_w, ada_b):
    depth, d, n = ada_w.shape
    tn = ADA_TN
    return pl.pallas_call(
        _ada_kernel,
        grid=(depth, n // tn),
        in_specs=[
            pl.BlockSpec((SUBLANES, d), lambda l, j: (0, 0)),
            pl.BlockSpec((None, d, tn), lambda l, j: (l, 0, j)),
            pl.BlockSpec((None, 1, tn), lambda l, j: (l, 0, j)),
        ],
        out_specs=pl.BlockSpec((None, SUBLANES, tn), lambda l, j: (l, 0, j)),
        out_shape=jax.ShapeDtypeStruct((depth, SUBLANES, n), F32),
        compiler_params=_cparams("parallel", "arbitrary"),
        name="ada_modulation",
    )(cvec, ada_w, ada_b.reshape(depth, 1, n))


def _norm_mod(x, g, mod_ref, shift_row, scale_row):
    y = x * _rms_scale(x, x.shape[-1]) * g
    return y * (1.0 + mod_ref[scale_row:scale_row + 1, :]) + mod_ref[shift_row:shift_row + 1, :]


def _sigmoid(x):
    return 0.5 * jnp.tanh(0.5 * x) + 0.5


def _norm_kernel(x_ref, g_ref, mod_ref, o_ref, *, shift_row, scale_row):
    o_ref[...] = _norm_mod(x_ref[...], g_ref[...], mod_ref, shift_row, scale_row).astype(o_ref.dtype)


def _norm_modulate(x, g, mod, *, shift_row, scale_row, tm):
    b, l, d = x.shape
    tm = min(tm, l)
    return pl.pallas_call(
        functools.partial(_norm_kernel, shift_row=shift_row, scale_row=scale_row),
        grid=(b, l // tm),
        in_specs=[
            pl.BlockSpec((None, tm, d), lambda bi, i: (bi, i, 0)),
            pl.BlockSpec((1, d), lambda bi, i: (0, 0)),
            pl.BlockSpec((None, 6, d), lambda bi, i: (bi, 0, 0)),
        ],
        out_specs=pl.BlockSpec((None, tm, d), lambda bi, i: (bi, i, 0)),
        out_shape=jax.ShapeDtypeStruct((b, l, d), BF16),
        compiler_params=_cparams("parallel", "parallel"),
        name="norm_modulate",
    )(x, g.reshape(1, d), mod)


def _norm_halo_kernel(xp_ref, xc_ref, xn_ref, g_ref, mod_ref, o_ref, *, tm, nt, shift_row, scale_row):
    ti = pl.program_id(1)
    halo = BF16_ROWS
    g = g_ref[...]
    o_ref[halo:halo + tm, :] = _norm_mod(xc_ref[...], g, mod_ref, shift_row, scale_row).astype(o_ref.dtype)
    prev = jnp.where(ti > 0, _norm_mod(xp_ref[...], g, mod_ref, shift_row, scale_row), 0.0)
    o_ref[0:halo, :] = prev.astype(o_ref.dtype)
    nxt = jnp.where(ti < nt - 1, _norm_mod(xn_ref[...], g, mod_ref, shift_row, scale_row), 0.0)
    o_ref[halo + tm:, :] = nxt.astype(o_ref.dtype)


def _norm_modulate_tiles(x, g, mod, *, shift_row, scale_row, tm):
    b, l, d = x.shape
    tm = min(tm, l)
    nt = l // tm
    hb = tm // BF16_ROWS
    rows = tm + 2 * BF16_ROWS
    return pl.pallas_call(
        functools.partial(_norm_halo_kernel, tm=tm, nt=nt, shift_row=shift_row, scale_row=scale_row),
        grid=(b, nt),
        in_specs=[
            pl.BlockSpec((None, BF16_ROWS, d), lambda bi, i: (bi, jnp.maximum(i * hb - 1, 0), 0)),
            pl.BlockSpec((None, tm, d), lambda bi, i: (bi, i, 0)),
            pl.BlockSpec((None, BF16_ROWS, d), lambda bi, i: (bi, jnp.minimum((i + 1) * hb, l // BF16_ROWS - 1), 0)),
            pl.BlockSpec((1, d), lambda bi, i: (0, 0)),
            pl.BlockSpec((None, 6, d), lambda bi, i: (bi, 0, 0)),
        ],
        out_specs=pl.BlockSpec((None, None, rows, d), lambda bi, i: (bi, i, 0, 0)),
        out_shape=jax.ShapeDtypeStruct((b, nt, rows, d), BF16),
        compiler_params=_cparams("parallel", "parallel"),
        name="norm_modulate_tiles",
    )(x, x, x, g.reshape(1, d), mod)


def _matmul_kernel(a_ref, w_ref, o_ref, *, sigmoid):
    z = jnp.dot(a_ref[...], w_ref[...], preferred_element_type=F32)
    if sigmoid:
        z = _sigmoid(z)
    o_ref[...] = z.astype(o_ref.dtype)


def _matmul(a, w, layer, *, sigmoid, out_dtype, tm, tn, name):
    b, l, k = a.shape
    n = w.shape[2]
    tm = min(tm, l)
    return pl.pallas_call(
        functools.partial(_matmul_kernel, sigmoid=sigmoid),
        grid=(b, l // tm, n // tn),
        in_specs=[
            pl.BlockSpec((None, tm, k), lambda bi, i, j: (bi, i, 0)),
            pl.BlockSpec((None, k, tn), lambda bi, i, j: (layer, 0, j)),
        ],
        out_specs=pl.BlockSpec((None, tm, tn), lambda bi, i, j: (bi, i, j)),
        out_shape=jax.ShapeDtypeStruct((b, l, n), out_dtype),
        compiler_params=_cparams("parallel", "parallel", "arbitrary"),
        name=name,
    )(a, w)


def _pool_kernel(prev_ref, cur_ref, next_ref, pw_ref, ps_ref, o_ref, ext_ref, *, tm, nt, seq_len):
    ti = pl.program_id(1)
    cur = cur_ref[...]
    ext_ref[SUBLANES:SUBLANES + tm, :] = cur
    ext_ref[0:SUBLANES, :] = jnp.where(ti > 0, prev_ref[...], 0.0)
    ext_ref[SUBLANES + tm:, :] = jnp.where(ti < nt - 1, next_ref[...], 0.0)
    t = ti * tm + lax.broadcasted_iota(jnp.int32, (tm, 1), 0)
    for g, w in enumerate(POOL_WINDOWS):
        r = w // 2
        c0 = g * POOL_GROUP_W
        s = ext_ref[SUBLANES - r:SUBLANES - r + tm, c0:c0 + POOL_GROUP_W]
        for dlt in range(-r + 1, r):
            s = s + ext_ref[SUBLANES + dlt:SUBLANES + dlt + tm, c0:c0 + POOL_GROUP_W]
        cnt = (jnp.minimum(t + r, seq_len) - jnp.maximum(t - r, 0)).astype(F32)
        pooled = s / cnt - cur[:, c0:c0 + POOL_GROUP_W]
        y = jnp.dot(pooled.astype(BF16), pw_ref[g], preferred_element_type=F32)
        o_ref[:, c0:c0 + POOL_GROUP_W] = (y * ps_ref[:, c0:c0 + POOL_GROUP_W]).astype(o_ref.dtype)


def _pool_mix(zs, pool_w, pool_scale, *, tm):
    b, l, _ = zs.shape
    tm = min(tm, l)
    nt = l // tm
    hb = tm // SUBLANES
    col = ZS_POOL // POOL_W
    return pl.pallas_call(
        functools.partial(_pool_kernel, tm=tm, nt=nt, seq_len=l),
        grid=(b, nt),
        in_specs=[
            pl.BlockSpec((None, SUBLANES, POOL_W), lambda bi, i: (bi, jnp.maximum(i * hb - 1, 0), col)),
            pl.BlockSpec((None, tm, POOL_W), lambda bi, i: (bi, i, col)),
            pl.BlockSpec((None, SUBLANES, POOL_W),
                         lambda bi, i: (bi, jnp.minimum((i + 1) * hb, l // SUBLANES - 1), col)),
            pl.BlockSpec((len(POOL_WINDOWS), POOL_GROUP_W, POOL_GROUP_W), lambda bi, i: (0, 0, 0)),
            pl.BlockSpec((1, POOL_W), lambda bi, i: (0, 0)),
        ],
        out_specs=pl.BlockSpec((None, tm, POOL_W), lambda bi, i: (bi, i, 0)),
        out_shape=jax.ShapeDtypeStruct((b, l, POOL_W), BF16),
        scratch_shapes=[pltpu.VMEM((tm + 2 * SUBLANES, POOL_W), F32)],
        compiler_params=_cparams("parallel", "parallel"),
        name="pool_mix",
    )(zs, zs, zs, pool_w, pool_scale.reshape(1, POOL_W))


def _sg_kernel(z_ref, g_ref, w_ref, b_ref, o_ref, *, tm):
    z = z_ref[...]
    a = z * (0.5 * (1.0 + jnp.tanh(math.sqrt(2.0 / math.pi) * (z + 0.044715 * (z * z * z)))))
    u = a[:, :SG_W]
    v = a[:, SG_W:]
    vc = v - jnp.mean(v, axis=-1, keepdims=True)
    vn = (vc * lax.rsqrt(jnp.mean(vc * vc, axis=-1, keepdims=True) + EPS) * g_ref[...]).astype(BF16)
    for c in range(tm // CHUNK):
        rows = slice(c * CHUNK, (c + 1) * CHUNK)
        for g in range(SG_GROUPS):
            cols = slice(g * SG_GROUP_W, (g + 1) * SG_GROUP_W)
            s = jnp.dot(w_ref[g], vn[rows, cols], preferred_element_type=F32) + b_ref[g]
            o_ref[rows, cols] = (u[rows, cols] * s).astype(o_ref.dtype)


def _spatial_gating(zs, sg_norm_g, sg_w, sg_b, *, tm):
    b, l, _ = zs.shape
    tm = min(tm, l)
    return pl.pallas_call(
        functools.partial(_sg_kernel, tm=tm),
        grid=(b, l // tm),
        in_specs=[
            pl.BlockSpec((None, tm, 2 * SG_W), lambda bi, i: (bi, i, ZS_SG // (2 * SG_W))),
            pl.BlockSpec((1, SG_W), lambda bi, i: (0, 0)),
            pl.BlockSpec((SG_GROUPS, CHUNK, CHUNK), lambda bi, i: (0, 0, 0)),
            pl.BlockSpec((SG_GROUPS, CHUNK, 1), lambda bi, i: (0, 0, 0)),
        ],
        out_specs=pl.BlockSpec((None, tm, SG_W), lambda bi, i: (bi, i, 0)),
        out_shape=jax.ShapeDtypeStruct((b, l, SG_W), BF16),
        compiler_params=_cparams("parallel", "parallel"),
        name="spatial_gating",
    )(zs, sg_norm_g.reshape(1, SG_W), sg_w, sg_b.reshape(SG_GROUPS, CHUNK, 1))


def _q_kernel(z_ref, gl_ref, w_ref, gn_ref, gr_ref, grs_ref, cos_ref, sin_ref, o_ref, *, out_scale):
    z = z_ref[...]
    zn = (z * _rms_scale(z, Q_LORA) * gl_ref[...]).astype(BF16)
    q = jnp.dot(zn, w_ref[...], preferred_element_type=F32)
    nope_w = N_HEADS * QK_NOPE
    rope_w = N_HEADS * QK_ROPE
    lo = lax.broadcasted_iota(jnp.int32, (1, LANES), 1) < QK_ROPE
    cos = cos_ref[...]
    sin = sin_ref[...]
    gn = gn_ref[...] * out_scale
    for pair in range(N_HEADS // 2):
        blk = slice(pair * LANES, (pair + 1) * LANES)
        r2 = q[:, nope_w + pair * LANES:nope_w + (pair + 1) * LANES]
        r2s = q[:, nope_w + rope_w + pair * LANES:nope_w + rope_w + (pair + 1) * LANES]
        sq = r2 * r2
        ss_lo = jnp.sum(jnp.where(lo, sq, 0.0), axis=-1, keepdims=True)
        ss_hi = jnp.sum(jnp.where(lo, 0.0, sq), axis=-1, keepdims=True)
        rot = r2 * gr_ref[:, blk] * cos + r2s * grs_ref[:, blk] * sin
        for half, ss_r in enumerate((ss_lo, ss_hi)):
            h = 2 * pair + half
            nope = q[:, h * QK_NOPE:(h + 1) * QK_NOPE]
            inv = lax.rsqrt((jnp.sum(nope * nope, axis=-1, keepdims=True) + ss_r) * (1.0 / QK_HEAD) + EPS)
            o_ref[h, :, 0:LANES] = (nope * inv * gn).astype(o_ref.dtype)
            keep = lo if half == 0 else jnp.logical_not(lo)
            o_ref[h, :, LANES:HEAD_PAD] = jnp.where(keep, rot * (inv * out_scale), 0.0).astype(o_ref.dtype)


def _q_proj(zs, q_lat_g, w_q, gn, gr, grs, cos_tab, sin_tab, *, out_scale, tm):
    b, l, _ = zs.shape
    tm = min(tm, l)
    return pl.pallas_call(
        functools.partial(_q_kernel, out_scale=out_scale),
        grid=(b, l // tm),
        in_specs=[
            pl.BlockSpec((None, tm, Q_LORA), lambda bi, i: (bi, i, ZS_Q // Q_LORA)),
            pl.BlockSpec((1, Q_LORA), lambda bi, i: (0, 0)),
            pl.BlockSpec(w_q.shape, lambda bi, i: (0, 0)),
            pl.BlockSpec((1, QK_NOPE), lambda bi, i: (0, 0)),
            pl.BlockSpec((1, N_HEADS * QK_ROPE), lambda bi, i: (0, 0)),
            pl.BlockSpec((1, N_HEADS * QK_ROPE), lambda bi, i: (0, 0)),
            pl.BlockSpec((tm, LANES), lambda bi, i: (i, 0)),
            pl.BlockSpec((tm, LANES), lambda bi, i: (i, 0)),
        ],
        out_specs=pl.BlockSpec((None, N_HEADS, tm, HEAD_PAD), lambda bi, i: (bi, 0, i, 0)),
        out_shape=jax.ShapeDtypeStruct((b, N_HEADS, l, HEAD_PAD), BF16),
        compiler_params=_cparams("parallel", "parallel"),
        name="q_proj",
    )(zs, q_lat_g.reshape(1, Q_LORA), w_q, gn, gr, grs, cos_tab, sin_tab)


def _kv_kernel(z_ref, gl_ref, w_ref, gn_ref, gt_ref, tab_ref, k_ref, v_ref):
    z = z_ref[...]
    zkv = z[:, :KV_LORA]
    zn = (zkv * _rms_scale(zkv, KV_LORA) * gl_ref[...]).astype(BF16)
    kv = jnp.dot(zn, w_ref[...], preferred_element_type=F32)
    lo = lax.broadcasted_iota(jnp.int32, (1, LANES), 1) < QK_ROPE
    kr2 = z[:, KV_LORA:KV_LORA + LANES]
    ss_r = jnp.sum(jnp.where(lo, kr2 * kr2, 0.0), axis=-1, keepdims=True)
    t = kr2 * gt_ref[...] * tab_ref[...]
    rot = t + pltpu.roll(t, QK_ROPE, axis=1)
    gn = gn_ref[...]
    ones = jnp.ones((z.shape[0], LANES), v_ref.dtype)
    for h in range(N_HEADS):
        nope = kv[:, h * QK_NOPE:(h + 1) * QK_NOPE]
        inv = lax.rsqrt((jnp.sum(nope * nope, axis=-1, keepdims=True) + ss_r) * (1.0 / QK_HEAD) + EPS)
        k_ref[h, :, 0:LANES] = (nope * inv * gn).astype(k_ref.dtype)
        keep = lo if h % 2 == 0 else jnp.logical_not(lo)
        k_ref[h, :, LANES:HEAD_PAD] = jnp.where(keep, rot * inv, 0.0).astype(k_ref.dtype)
        v_ref[h, :, 0:LANES] = kv[:, N_HEADS * QK_NOPE + h * V_HEAD:N_HEADS * QK_NOPE + (h + 1) * V_HEAD].astype(
            v_ref.dtype)
        v_ref[h, :, LANES:HEAD_PAD] = ones


def _kv_proj(zs, kv_lat_g, w_kv, gn, gt, tab, *, tm):
    b, l, _ = zs.shape
    tm = min(tm, l)
    blk = KV_LORA + 2 * LANES
    shape = jax.ShapeDtypeStruct((b, N_HEADS, l, HEAD_PAD), BF16)
    spec = pl.BlockSpec((None, N_HEADS, tm, HEAD_PAD), lambda bi, i: (bi, 0, i, 0))
    return pl.pallas_call(
        _kv_kernel,
        grid=(b, l // tm),
        in_specs=[
            pl.BlockSpec((None, tm, blk), lambda bi, i: (bi, i, ZS_KV // blk)),
            pl.BlockSpec((1, KV_LORA), lambda bi, i: (0, 0)),
            pl.BlockSpec(w_kv.shape, lambda bi, i: (0, 0)),
            pl.BlockSpec((1, QK_NOPE), lambda bi, i: (0, 0)),
            pl.BlockSpec((1, LANES), lambda bi, i: (0, 0)),
            pl.BlockSpec((tm, LANES), lambda bi, i: (i, 0)),
        ],
        out_specs=[spec, spec],
        out_shape=[shape, shape],
        compiler_params=_cparams("parallel", "parallel"),
        name="kv_proj",
    )(zs, kv_lat_g.reshape(1, KV_LORA), w_kv, gn, gt, tab)


def _attn_kernel(q_ref, *refs, chunks):
    n_src = len(chunks)
    o_ref, acc_ref, m_ref = refs[2 * n_src:]
    q = q_ref[...]
    acc_ref[...] = jnp.zeros_like(acc_ref)
    m_ref[...] = jnp.full_like(m_ref, -jnp.inf)

    def step(k, v):
        s = lax.dot_general(q, k, (((1,), (1,)), ((), ())), preferred_element_type=F32)
        m_prev = m_ref[...]
        m_new = jnp.maximum(m_prev, jnp.max(s, axis=-1, keepdims=True))
        alpha = jnp.exp2(m_prev - m_new)
        p = jnp.exp2(s - jnp.concatenate([m_new] * (s.shape[1] // LANES), axis=1)).astype(BF16)
        pv = jnp.dot(p, v, preferred_element_type=F32)
        acc_ref[...] = acc_ref[...] * jnp.concatenate([alpha, alpha], axis=1) + pv
        m_ref[...] = m_new

    for si, (bk, n) in enumerate(chunks):
        k_ref, v_ref = refs[2 * si], refs[2 * si + 1]
        if n == 1:
            step(k_ref[...], v_ref[...])
        else:
            def body(c, carry, k_ref=k_ref, v_ref=v_ref, bk=bk):
                off = pl.multiple_of(c * bk, bk)
                step(k_ref[pl.ds(off, bk), :], v_ref[pl.ds(off, bk), :])
                return carry

            lax.fori_loop(0, n, body, 0, unroll=ATT_UNROLL)
    acc = acc_ref[...]
    o_ref[...] = (acc[:, :V_HEAD] / acc[:, V_HEAD:]).astype(o_ref.dtype)


def _attention(q, sources, *, bq, bk):
    b, h, l, _ = q.shape
    bq = min(bq, l)
    chunks = []
    in_specs = [pl.BlockSpec((None, None, bq, HEAD_PAD), lambda bi, hi, i: (bi, hi, i, 0))]
    args = [q]
    for k, v in sources:
        lk = k.shape[2]
        cb = min(bk, lk)
        chunks.append((cb, lk // cb))
        for a in (k, v):
            in_specs.append(pl.BlockSpec((None, None, lk, HEAD_PAD), lambda bi, hi, i: (bi, hi, 0, 0)))
            args.append(a)
    return pl.pallas_call(
        functools.partial(_attn_kernel, chunks=tuple(chunks)),
        grid=(b, h, l // bq),
        in_specs=in_specs,
        out_specs=pl.BlockSpec((None, bq, V_HEAD), lambda bi, hi, i: (bi, i, hi)),
        out_shape=jax.ShapeDtypeStruct((b, l, h * V_HEAD), BF16),
        scratch_shapes=[pltpu.VMEM((bq, HEAD_PAD), F32), pltpu.VMEM((bq, LANES), F32)],
        compiler_params=_cparams("parallel", "parallel", "arbitrary"),
        name="latent_attention",
    )(*args)


def _merge_out_kernel(p_ref, a_ref, s_ref, gp_ref, ga_ref, gs_ref, wp_ref, wa_ref, ws_ref, wo_ref, x_ref, g1_ref,
                      o_ref, y_ref, *, nj, tn):
    j = pl.program_id(2)

    @pl.when(j < nj)
    def _():
        y = gp_ref[...].astype(F32) * jnp.dot(p_ref[...], wp_ref[j], preferred_element_type=F32)
        y += ga_ref[...].astype(F32) * jnp.dot(a_ref[...], wa_ref[j], preferred_element_type=F32)
        y += gs_ref[...].astype(F32) * jnp.dot(s_ref[...], ws_ref[j], preferred_element_type=F32)
        y_ref[j] = y.astype(y_ref.dtype)

    @pl.when(j >= nj)
    def _():
        wo = wo_ref.at[j - nj]
        acc = jnp.dot(y_ref[0], wo[0:tn, :], preferred_element_type=F32)
        for c in range(1, nj):
            acc += jnp.dot(y_ref[c], wo[c * tn:(c + 1) * tn, :], preferred_element_type=F32)
        o_ref[...] = x_ref[...] + g1_ref[...] * acc


def _merge_out(pool_out, attn, sg_out, gates, w_p, w_a, w_s, w_o, layer, x, mod, *, tm, tn):
    b, l, _ = pool_out.shape
    nj = w_p.shape[1]
    d = nj * tn
    tm = min(tm, l)

    def first(j):
        return jnp.minimum(j, nj - 1)

    def second(j):
        return jnp.maximum(j - nj, 0)

    def act(width):
        return pl.BlockSpec((None, tm, width), lambda bi, i, j: (bi, i, 0))

    def gate(branch):
        return pl.BlockSpec((None, tm, tn), lambda bi, i, j: (bi, i, branch * nj + first(j)))

    def wgt(width):
        return pl.BlockSpec((None, nj, width, tn), lambda bi, i, j: (layer, 0, 0, 0), pipeline_mode=pl.Buffered(1))

    return pl.pallas_call(
        functools.partial(_merge_out_kernel, nj=nj, tn=tn),
        grid=(b, l // tm, 2 * nj),
        in_specs=[act(POOL_W), act(N_HEADS * V_HEAD), act(SG_W), gate(0), gate(1), gate(2),
                  wgt(POOL_W), wgt(N_HEADS * V_HEAD), wgt(SG_W), wgt(d),
                  pl.BlockSpec((None, tm, tn), lambda bi, i, j: (bi, i, second(j))),
                  pl.BlockSpec((None, 1, tn), lambda bi, i, j: (bi, 0, second(j)))],
        out_specs=pl.BlockSpec((None, tm, tn), lambda bi, i, j: (bi, i, second(j))),
        out_shape=jax.ShapeDtypeStruct((b, l, d), F32),
        scratch_shapes=[pltpu.VMEM((nj, tm, tn), BF16)],
        compiler_params=_cparams("parallel", "parallel", "arbitrary"),
        name="merge_out_proj",
    )(pool_out, attn, sg_out, gates, gates, gates, w_p, w_a, w_s, w_o, x, mod[:, 2:3, :])


def _mm_res_kernel(a_ref, w_ref, x_ref, gate_ref, o_ref):
    y = jnp.dot(a_ref[...], w_ref[...], preferred_element_type=F32)
    o_ref[...] = x_ref[...] + gate_ref[...] * y


def _mm_residual(a, w, layer, x, mod, gate_row, *, tm, tn, name):
    b, l, k = a.shape
    d = w.shape[2]
    tm = min(tm, l)
    return pl.pallas_call(
        _mm_res_kernel,
        grid=(b, l // tm, d // tn),
        in_specs=[
            pl.BlockSpec((None, tm, k), lambda bi, i, j: (bi, i, 0)),
            pl.BlockSpec((None, k, tn), lambda bi, i, j: (layer, 0, j)),
            pl.BlockSpec((None, tm, tn), lambda bi, i, j: (bi, i, j)),
            pl.BlockSpec((None, 1, tn), lambda bi, i, j: (bi, 0, j)),
        ],
        out_specs=pl.BlockSpec((None, tm, tn), lambda bi, i, j: (bi, i, j)),
        out_shape=jax.ShapeDtypeStruct((b, l, d), F32),
        compiler_params=_cparams("parallel", "parallel", "arbitrary"),
        name=name,
    )(a, w, x, mod[:, gate_row:gate_row + 1, :])


def _ffn_up_kernel(h_ref, wa_ref, wv_ref, cwa_ref, cwv_ref, cba_ref, cbv_ref, o_ref, *, tm):
    halo = BF16_ROWS
    h = h_ref[...]
    rows = h.shape[0]

    def conv(w_ref, cw_ref, cb_ref):
        u = jnp.dot(h, w_ref[...], preferred_element_type=F32)
        acc = cb_ref[...] + pltpu.roll(u, 1, axis=0) * cw_ref[0:1, :]
        acc = acc + u * cw_ref[1:2, :]
        acc = acc + pltpu.roll(u, rows - 1, axis=0) * cw_ref[2:3, :]
        return acc[halo:halo + tm, :]

    a = conv(wa_ref, cwa_ref, cba_ref)
    v = conv(wv_ref, cwv_ref, cbv_ref)
    o_ref[...] = (a * _sigmoid(a) * v).astype(o_ref.dtype)


def _ffn_up(h_tiles, w_up, layer, conv_w, conv_b, *, tf):
    b, nt, rows, d = h_tiles.shape
    tm = rows - 2 * BF16_ROWS
    l = nt * tm
    d_ff = w_up.shape[2] // 2
    nf = d_ff // tf
    return pl.pallas_call(
        functools.partial(_ffn_up_kernel, tm=tm),
        grid=(b, nt, nf),
        in_specs=[
            pl.BlockSpec((None, None, rows, d), lambda bi, i, j: (bi, i, 0, 0)),
            pl.BlockSpec((None, d, tf), lambda bi, i, j: (layer, 0, j)),
            pl.BlockSpec((None, d, tf), lambda bi, i, j: (layer, 0, nf + j)),
            pl.BlockSpec((CONV_W, tf), lambda bi, i, j: (0, j)),
            pl.BlockSpec((CONV_W, tf), lambda bi, i, j: (0, nf + j)),
            pl.BlockSpec((1, tf), lambda bi, i, j: (0, j)),
            pl.BlockSpec((1, tf), lambda bi, i, j: (0, nf + j)),
        ],
        out_specs=pl.BlockSpec((None, tm, tf), lambda bi, i, j: (bi, i, j)),
        out_shape=jax.ShapeDtypeStruct((b, l, d_ff), BF16),
        compiler_params=_cparams("parallel", "parallel", "arbitrary"),
        name="ffn_up_conv_gate",
    )(h_tiles, w_up, w_up, conv_w, conv_w, conv_b.reshape(1, -1), conv_b.reshape(1, -1))


def _rope_tables(n_tokens, rotate):
    if rotate:
        t = np.arange(n_tokens)
        inv_freq = ROPE_THETA ** (-np.arange(ROPE_FREQS, dtype=np.float64) / ROPE_FREQS)
        ar = (t // GRID_W)[:, None] * inv_freq
        ac = (t % GRID_W)[:, None] * inv_freq
        c = np.concatenate([np.cos(ar), np.cos(ar), np.cos(ac), np.cos(ac)], axis=1)
        s = np.concatenate([-np.sin(ar), np.sin(ar), -np.sin(ac), np.sin(ac)], axis=1)
    else:
        c, s = np.ones((n_tokens, QK_ROPE)), np.zeros((n_tokens, QK_ROPE))
    cat = lambda a, b_: jnp.asarray(np.concatenate([a, b_], axis=1), F32)
    return dict(q_cos=cat(c, c), q_sin=cat(s, s), k=cat(c, s))


def _stacked_weights(w_in, w_br_pool, w_br_mla, w_br_sg, w_o, ffn_up, ffn_down):
    depth, d, _ = w_in.shape
    o_q, o_kv, o_kr, o_sg, o_gate = POOL_W, POOL_W + Q_LORA, POOL_W + Q_LORA + KV_LORA, \
        POOL_W + Q_LORA + KV_LORA + QK_ROPE, POOL_W + Q_LORA + KV_LORA + QK_ROPE + 2 * SG_W
    w_kr = w_in[:, :, o_kr:o_sg]
    w_small = jnp.concatenate([
        w_in[:, :, :o_q], w_in[:, :, o_q:o_kv], w_in[:, :, o_sg:o_gate], w_in[:, :, o_kv:o_kr], w_kr,
        w_kr[:, :, ROPE_SWAP], jnp.zeros((depth, d, ZS_COLS - ZS_KV - KV_LORA - 2 * QK_ROPE), w_in.dtype)],
        axis=2).astype(BF16)
    def col_tiles(w):
        dep, k, n = w.shape
        return w.astype(BF16).reshape(dep, k, n // MERGE_TN, MERGE_TN).transpose(0, 2, 1, 3)

    return dict(
        w_small=w_small, w_gate=w_in[:, :, o_gate:].astype(BF16), w_br_pool=col_tiles(w_br_pool),
        w_br_mla=col_tiles(w_br_mla), w_br_sg=col_tiles(w_br_sg), w_o=col_tiles(w_o),
        ffn_up=ffn_up.astype(BF16), ffn_down=ffn_down.astype(BF16))


def _layer_weights(p):
    w_uq = p["w_uq"].reshape(Q_LORA, N_HEADS, QK_HEAD)
    w_q_rope = w_uq[:, :, QK_NOPE:]
    w_q = jnp.concatenate([
        w_uq[:, :, :QK_NOPE].reshape(Q_LORA, -1), w_q_rope.reshape(Q_LORA, -1),
        w_q_rope[:, :, ROPE_SWAP].reshape(Q_LORA, -1)], axis=1).astype(BF16)
    gq = p["q_norm_g"]
    gq_n = gq[:QK_NOPE].reshape(1, QK_NOPE)
    gq_r = jnp.tile(gq[QK_NOPE:], N_HEADS).reshape(1, -1)
    gq_rs = jnp.tile(gq[QK_NOPE:][ROPE_SWAP], N_HEADS).reshape(1, -1)

    w_ukv = p["w_ukv"].reshape(KV_LORA, N_HEADS, QK_NOPE + V_HEAD)
    w_kv = jnp.concatenate([
        w_ukv[:, :, :QK_NOPE].reshape(KV_LORA, -1), w_ukv[:, :, QK_NOPE:].reshape(KV_LORA, -1)], axis=1).astype(BF16)
    gk = p["k_norm_g"]
    gk_n = gk[:QK_NOPE].reshape(1, QK_NOPE)
    gk_t = jnp.concatenate([gk[QK_NOPE:], gk[QK_NOPE:][ROPE_SWAP]]).reshape(1, LANES)
    return dict(
        w_q=w_q, gq_n=gq_n, gq_r=gq_r, gq_rs=gq_rs, w_kv=w_kv, gk_n=gk_n, gk_t=gk_t,
        pool_w=p["pool_w"].astype(BF16), sg_w=p["sg_w"].astype(BF16))


def _keys_values(zs, p, w, tabs):
    return _kv_proj(zs, p["kv_lat_g"], w["w_kv"], w["gk_n"], w["gk_t"], tabs["k"], tm=MIXER_TM)


def _queries(zs, p, w, tabs):
    out_scale = QK_HEAD ** -0.5 * math.log2(math.e)
    return _q_proj(zs, p["q_lat_g"], w["w_q"], w["gq_n"], w["gq_r"], w["gq_rs"], tabs["q_cos"], tabs["q_sin"],
                   out_scale=out_scale, tm=MIXER_TM)


def _in_proj(x, p, ws, layer, mod, *, gates):
    h = _norm_modulate(x, p["norm1_g"], mod, shift_row=0, scale_row=1, tm=DENSE_TM)
    zs = _matmul(h, ws["w_small"], layer, sigmoid=False, out_dtype=F32, tm=DENSE_TM, tn=SMALL_TN, name="in_proj_small")
    g = None
    if gates:
        g = _matmul(h, ws["w_gate"], layer, sigmoid=True, out_dtype=BF16, tm=DENSE_TM, tn=GATE_TN,
                    name="in_proj_gate")
    return zs, g


def _mix_and_ffn(x, zs, gates, attn, p, w, ws, layer, mod):
    pool_out = _pool_mix(zs, w["pool_w"], p["pool_scale"], tm=MIXER_TM)
    sg_out = _spatial_gating(zs, p["sg_norm_g"], w["sg_w"], p["sg_b"], tm=MIXER_TM)
    x = _merge_out(pool_out, attn, sg_out, gates, ws["w_br_pool"], ws["w_br_mla"], ws["w_br_sg"], ws["w_o"], layer,
                   x, mod, tm=DENSE_TM, tn=MERGE_TN)
    h2 = _norm_modulate_tiles(x, p["norm2_g"], mod, shift_row=3, scale_row=4, tm=DENSE_TM)
    act = _ffn_up(h2, ws["ffn_up"], layer, p["ffn_conv_w"], p["ffn_conv_b"], tf=FFN_TF)
    return _mm_residual(act, ws["ffn_down"], layer, x, mod, 5, tm=DENSE_TM, tn=DOWN_TN, name="ffn_down_residual")


def kernel(x, c, ctx, c_ctx, ada_w, ada_b, norm1_g, w_in, pool_w, pool_scale, q_lat_g, w_uq, kv_lat_g, w_ukv, q_norm_g, k_norm_g, sg_norm_g, sg_w, sg_b, w_br_pool, w_br_mla, w_br_sg, w_o, norm2_g, ffn_up, ffn_conv_w, ffn_conv_b, ffn_down):
    b, l, d = x.shape
    lc = ctx.shape[1]
    depth = ada_w.shape[0]
    assert b + 1 <= SUBLANES

    cvec = jnp.concatenate([c, c_ctx[None, :], jnp.zeros((SUBLANES - b - 1, d), c.dtype)], axis=0)
    mod = _ada(cvec, ada_w, ada_b).reshape(depth, SUBLANES, 6, d)
    rope_lat = _rope_tables(l, rotate=True)
    rope_ctx = _rope_tables(lc, rotate=False)
    ws = _stacked_weights(w_in, w_br_pool, w_br_mla, w_br_sg, w_o, ffn_up, ffn_down)

    xl, xc = x, ctx
    for i in range(depth):
        p = {
            "norm1_g": norm1_g[i], "pool_w": pool_w[i], "pool_scale": pool_scale[i],
            "q_lat_g": q_lat_g[i], "w_uq": w_uq[i], "kv_lat_g": kv_lat_g[i], "w_ukv": w_ukv[i],
            "q_norm_g": q_norm_g[i], "k_norm_g": k_norm_g[i], "sg_norm_g": sg_norm_g[i], "sg_w": sg_w[i],
            "sg_b": sg_b[i], "norm2_g": norm2_g[i], "ffn_conv_w": ffn_conv_w[i], "ffn_conv_b": ffn_conv_b[i],
        }
        w = _layer_weights(p)
        mod_lat = mod[i, :b]
        mod_ctx = jnp.broadcast_to(mod[i, b:b + 1], (b, 6, d))

        zl, gl = _in_proj(xl, p, ws, i, mod_lat, gates=True)
        zc, gc = _in_proj(xc, p, ws, i, mod_ctx, gates=i < depth - 1)
        kc, vc = _keys_values(zc, p, w, rope_ctx)
        kl, vl = _keys_values(zl, p, w, rope_lat)
        ql = _queries(zl, p, w, rope_lat)
        attn_l = _attention(ql, [(kl, vl), (kc, vc)], bq=ATT_BQ, bk=ATT_BK)
        xl = _mix_and_ffn(xl, zl, gl, attn_l, p, w, ws, i, mod_lat)
        if i < depth - 1:
            qc = _queries(zc, p, w, rope_ctx)
            attn_c = _attention(qc, [(kc, vc)], bq=ATT_BQ, bk=ATT_BK)
            xc = _mix_and_ffn(xc, zc, gc, attn_c, p, w, ws, i, mod_ctx)
    return xl
```

```python
import functools
import math

import jax
import jax.numpy as jnp
import numpy as np
from jax import lax
from jax.experimental import pallas as pl
from jax.experimental.pallas import tpu as pltpu

F32 = jnp.float32
BF16 = jnp.bfloat16

GRID_W = 64
EPS = 1e-6
POOL_W = 512
POOL_WINDOWS = (2, 4, 8, 16)
POOL_GROUP_W = POOL_W // len(POOL_WINDOWS)
N_HEADS = 8
Q_LORA = 512
KV_LORA = 256
QK_NOPE = 128
QK_ROPE = 64
QK_HEAD = QK_NOPE + QK_ROPE
V_HEAD = 128
ROPE_FREQS = QK_ROPE // 4
ROPE_THETA = 10000.0
SG_W = 512
SG_GROUPS = 4
SG_GROUP_W = SG_W // SG_GROUPS
CHUNK = 128
CONV_W = 3

LANES = 128
SUBLANES = 8
BF16_ROWS = 16
HEAD_PAD = 2 * LANES

VMEM_LIMIT_BYTES = 56 * 1024 * 1024
ADA_TN = 1024
DENSE_TM = 1024
MIXER_TM = 1024
SMALL_TN = 2560
GATE_TN = 2048
MERGE_TN = 512
FFN_TF = 512
DOWN_TN = 512
ATT_BQ = 2048
ATT_BK = 512
ATT_UNROLL = 16

ZS_POOL = 0
ZS_Q = 512
ZS_SG = 1024
ZS_KV = 2048
ZS_COLS = 2560
ROPE_SWAP = np.concatenate([np.arange(16, 32), np.arange(0, 16), np.arange(48, 64), np.arange(32, 48)])


def _cparams(*sem):
    return pltpu.CompilerParams(dimension_semantics=sem, vmem_limit_bytes=VMEM_LIMIT_BYTES)


def _rms_scale(x, width):
    return lax.rsqrt(jnp.sum(x * x, axis=-1, keepdims=True) * (1.0 / width) + EPS)


def _ada_kernel(c_ref, w_ref, b_ref, o_ref):
    c = c_ref[...]
    s = c * jax.nn.sigmoid(c)
    o_ref[...] = jnp.dot(s, w_ref[...], preferred_element_type=F32, precision=lax.Precision.HIGHEST) + b_ref[...]


def _ada(cvec, ada_w, ada_b):
    depth, d, n = ada_w.shape
    tn = ADA_TN
    return pl.pallas_call(
        _ada_kernel,
        grid=(depth, n // tn),
        in_specs=[
            pl.BlockSpec((SUBLANES, d), lambda l, j: (0, 0)),
            pl.BlockSpec((None, d, tn), lambda l, j: (l, 0, j)),
            pl.BlockSpec((None, 1, tn), lambda l, j: (l, 0, j)),
        ],
        out_specs=pl.BlockSpec((None, SUBLANES, tn), lambda l, j: (l, 0, j)),
        out_shape=jax.ShapeDtypeStruct((depth, SUBLANES, n), F32),
        compiler_params=_cparams("parallel", "arbitrary"),
        name="ada_modulation",
    )(cvec, ada_w, ada_b.reshape(depth, 1, n))


def _norm_mod(x, g, mod_ref, shift_row, scale_row):
    y = x * _rms_scale(x, x.shape[-1]) * g
    return y * (1.0 + mod_ref[scale_row:scale_row + 1, :]) + mod_ref[shift_row:shift_row + 1, :]


def _sigmoid(x):
    return 0.5 * jnp.tanh(0.5 * x) + 0.5


def _norm_kernel(x_ref, g_ref, mod_ref, o_ref, *, shift_row, scale_row):
    o_ref[...] = _norm_mod(x_ref[...], g_ref[...], mod_ref, shift_row, scale_row).astype(o_ref.dtype)


def _norm_modulate(x, g, mod, *, shift_row, scale_row, tm):
    b, l, d = x.shape
    tm = min(tm, l)
    return pl.pallas_call(
        functools.partial(_norm_kernel, shift_row=shift_row, scale_row=scale_row),
        grid=(b, l // tm),
        in_specs=[
            pl.BlockSpec((None, tm, d), lambda bi, i: (bi, i, 0)),
            pl.BlockSpec((1, d), lambda bi, i: (0, 0)),
            pl.BlockSpec((None, 6, d), lambda bi, i: (bi, 0, 0)),
        ],
        out_specs=pl.BlockSpec((None, tm, d), lambda bi, i: (bi, i, 0)),
        out_shape=jax.ShapeDtypeStruct((b, l, d), BF16),
        compiler_params=_cparams("parallel", "parallel"),
        name="norm_modulate",
    )(x, g.reshape(1, d), mod)


def _norm_halo_kernel(xp_ref, xc_ref, xn_ref, g_ref, mod_ref, o_ref, *, tm, nt, shift_row, scale_row):
    ti = pl.program_id(1)
    halo = BF16_ROWS
    g = g_ref[...]
    o_ref[halo:halo + tm, :] = _norm_mod(xc_ref[...], g, mod_ref, shift_row, scale_row).astype(o_ref.dtype)
    prev = jnp.where(ti > 0, _norm_mod(xp_ref[...], g, mod_ref, shift_row, scale_row), 0.0)
    o_ref[0:halo, :] = prev.astype(o_ref.dtype)
    nxt = jnp.where(ti < nt - 1, _norm_mod(xn_ref[...], g, mod_ref, shift_row, scale_row), 0.0)
    o_ref[halo + tm:, :] = nxt.astype(o_ref.dtype)


def _norm_modulate_tiles(x, g, mod, *, shift_row, scale_row, tm):
    b, l, d = x.shape
    tm = min(tm, l)
    nt = l // tm
    hb = tm // BF16_ROWS
    rows = tm + 2 * BF16_ROWS
    return pl.pallas_call(
        functools.partial(_norm_halo_kernel, tm=tm, nt=nt, shift_row=shift_row, scale_row=scale_row),
        grid=(b, nt),
        in_specs=[
            pl.BlockSpec((None, BF16_ROWS, d), lambda bi, i: (bi, jnp.maximum(i * hb - 1, 0), 0)),
            pl.BlockSpec((None, tm, d), lambda bi, i: (bi, i, 0)),
            pl.BlockSpec((None, BF16_ROWS, d), lambda bi, i: (bi, jnp.minimum((i + 1) * hb, l // BF16_ROWS - 1), 0)),
            pl.BlockSpec((1, d), lambda bi, i: (0, 0)),
            pl.BlockSpec((None, 6, d), lambda bi, i: (bi, 0, 0)),
        ],
        out_specs=pl.BlockSpec((None, None, rows, d), lambda bi, i: (bi, i, 0, 0)),
        out_shape=jax.ShapeDtypeStruct((b, nt, rows, d), BF16),
        compiler_params=_cparams("parallel", "parallel"),
        name="norm_modulate_tiles",
    )(x, x, x, g.reshape(1, d), mod)


def _matmul_kernel(a_ref, w_ref, o_ref, *, sigmoid):
    z = jnp.dot(a_ref[...], w_ref[...], preferred_element_type=F32)
    if sigmoid:
        z = _sigmoid(z)
    o_ref[...] = z.astype(o_ref.dtype)


def _matmul(a, w, layer, *, sigmoid, out_dtype, tm, tn, name):
    b, l, k = a.shape
    n = w.shape[2]
    tm = min(tm, l)
    return pl.pallas_call(
        functools.partial(_matmul_kernel, sigmoid=sigmoid),
        grid=(b, l // tm, n // tn),
        in_specs=[
            pl.BlockSpec((None, tm, k), lambda bi, i, j: (bi, i, 0)),
            pl.BlockSpec((None, k, tn), lambda bi, i, j: (layer, 0, j)),
        ],
        out_specs=pl.BlockSpec((None, tm, tn), lambda bi, i, j: (bi, i, j)),
        out_shape=jax.ShapeDtypeStruct((b, l, n), out_dtype),
        compiler_params=_cparams("parallel", "parallel", "arbitrary"),
        name=name,
    )(a, w)


def _pool_kernel(prev_ref, cur_ref, next_ref, pw_ref, ps_ref, o_ref, ext_ref, *, tm, nt, seq_len):
    ti = pl.program_id(1)
    cur = cur_ref[...]
    ext_ref[SUBLANES:SUBLANES + tm, :] = cur
    ext_ref[0:SUBLANES, :] = jnp.where(ti > 0, prev_ref[...], 0.0)
    ext_ref[SUBLANES + tm:, :] = jnp.where(ti < nt - 1, next_ref[...], 0.0)
    t = ti * tm + lax.broadcasted_iota(jnp.int32, (tm, 1), 0)
    for g, w in enumerate(POOL_WINDOWS):
        r = w // 2
        c0 = g * POOL_GROUP_W
        s = ext_ref[SUBLANES - r:SUBLANES - r + tm, c0:c0 + POOL_GROUP_W]
        for dlt in range(-r + 1, r):
            s = s + ext_ref[SUBLANES + dlt:SUBLANES + dlt + tm, c0:c0 + POOL_GROUP_W]
        cnt = (jnp.minimum(t + r, seq_len) - jnp.maximum(t - r, 0)).astype(F32)
        pooled = s / cnt - cur[:, c0:c0 + POOL_GROUP_W]
        y = jnp.dot(pooled.astype(BF16), pw_ref[g], preferred_element_type=F32)
        o_ref[:, c0:c0 + POOL_GROUP_W] = (y * ps_ref[:, c0:c0 + POOL_GROUP_W]).astype(o_ref.dtype)


def _pool_mix(zs, pool_w, pool_scale, *, tm):
    b, l, _ = zs.shape
    tm = min(tm, l)
    nt = l // tm
    hb = tm // SUBLANES
    col = ZS_POOL // POOL_W
    return pl.pallas_call(
        functools.partial(_pool_kernel, tm=tm, nt=nt, seq_len=l),
        grid=(b, nt),
        in_specs=[
            pl.BlockSpec((None, SUBLANES, POOL_W), lambda bi, i: (bi, jnp.maximum(i * hb - 1, 0), col)),
            pl.BlockSpec((None, tm, POOL_W), lambda bi, i: (bi, i, col)),
            pl.BlockSpec((None, SUBLANES, POOL_W),
                         lambda bi, i: (bi, jnp.minimum((i + 1) * hb, l // SUBLANES - 1), col)),
            pl.BlockSpec((len(POOL_WINDOWS), POOL_GROUP_W, POOL_GROUP_W), lambda bi, i: (0, 0, 0)),
            pl.BlockSpec((1, POOL_W), lambda bi, i: (0, 0)),
        ],
        out_specs=pl.BlockSpec((None, tm, POOL_W), lambda bi, i: (bi, i, 0)),
        out_shape=jax.ShapeDtypeStruct((b, l, POOL_W), BF16),
        scratch_shapes=[pltpu.VMEM((tm + 2 * SUBLANES, POOL_W), F32)],
        compiler_params=_cparams("parallel", "parallel"),
        name="pool_mix",
    )(zs, zs, zs, pool_w, pool_scale.reshape(1, POOL_W))


def _sg_kernel(z_ref, g_ref, w_ref, b_ref, o_ref, *, tm):
    z = z_ref[...]
    a = z * (0.5 * (1.0 + jnp.tanh(math.sqrt(2.0 / math.pi) * (z + 0.044715 * (z * z * z)))))
    u = a[:, :SG_W]
    v = a[:, SG_W:]
    vc = v - jnp.mean(v, axis=-1, keepdims=True)
    vn = (vc * lax.rsqrt(jnp.mean(vc * vc, axis=-1, keepdims=True) + EPS) * g_ref[...]).astype(BF16)
    for c in range(tm // CHUNK):
        rows = slice(c * CHUNK, (c + 1) * CHUNK)
        for g in range(SG_GROUPS):
            cols = slice(g * SG_GROUP_W, (g + 1) * SG_GROUP_W)
            s = jnp.dot(w_ref[g], vn[rows, cols], preferred_element_type=F32) + b_ref[g]
            o_ref[rows, cols] = (u[rows, cols] * s).astype(o_ref.dtype)


def _spatial_gating(zs, sg_norm_g, sg_w, sg_b, *, tm):
    b, l, _ = zs.shape
    tm = min(tm, l)
    return pl.pallas_call(
        functools.partial(_sg_kernel, tm=tm),
        grid=(b, l // tm),
        in_specs=[
            pl.BlockSpec((None, tm, 2 * SG_W), lambda bi, i: (bi, i, ZS_SG // (2 * SG_W))),
            pl.BlockSpec((1, SG_W), lambda bi, i: (0, 0)),
            pl.BlockSpec((SG_GROUPS, CHUNK, CHUNK), lambda bi, i: (0, 0, 0)),
            pl.BlockSpec((SG_GROUPS, CHUNK, 1), lambda bi, i: (0, 0, 0)),
        ],
        out_specs=pl.BlockSpec((None, tm, SG_W), lambda bi, i: (bi, i, 0)),
        out_shape=jax.ShapeDtypeStruct((b, l, SG_W), BF16),
        compiler_params=_cparams("parallel", "parallel"),
        name="spatial_gating",
    )(zs, sg_norm_g.reshape(1, SG_W), sg_w, sg_b.reshape(SG_GROUPS, CHUNK, 1))


def _q_kernel(z_ref, gl_ref, w_ref, gn_ref, gr_ref, grs_ref, cos_ref, sin_ref, o_ref, *, out_scale):
    z = z_ref[...]
    zn = (z * _rms_scale(z, Q_LORA) * gl_ref[...]).astype(BF16)
    q = jnp.dot(zn, w_ref[...], preferred_element_type=F32)
    nope_w = N_HEADS * QK_NOPE
    rope_w = N_HEADS * QK_ROPE
    lo = lax.broadcasted_iota(jnp.int32, (1, LANES), 1) < QK_ROPE
    cos = cos_ref[...]
    sin = sin_ref[...]
    gn = gn_ref[...] * out_scale
    for pair in range(N_HEADS // 2):
        blk = slice(pair * LANES, (pair + 1) * LANES)
        r2 = q[:, nope_w + pair * LANES:nope_w + (pair + 1) * LANES]
        r2s = q[:, nope_w + rope_w + pair * LANES:nope_w + rope_w + (pair + 1) * LANES]
        sq = r2 * r2
        ss_lo = jnp.sum(jnp.where(lo, sq, 0.0), axis=-1, keepdims=True)
        ss_hi = jnp.sum(jnp.where(lo, 0.0, sq), axis=-1, keepdims=True)
        rot = r2 * gr_ref[:, blk] * cos + r2s * grs_ref[:, blk] * sin
        for half, ss_r in enumerate((ss_lo, ss_hi)):
            h = 2 * pair + half
            nope = q[:, h * QK_NOPE:(h + 1) * QK_NOPE]
            inv = lax.rsqrt((jnp.sum(nope * nope, axis=-1, keepdims=True) + ss_r) * (1.0 / QK_HEAD) + EPS)
            o_ref[h, :, 0:LANES] = (nope * inv * gn).astype(o_ref.dtype)
            keep = lo if half == 0 else jnp.logical_not(lo)
            o_ref[h, :, LANES:HEAD_PAD] = jnp.where(keep, rot * (inv * out_scale), 0.0).astype(o_ref.dtype)


def _q_proj(zs, q_lat_g, w_q, gn, gr, grs, cos_tab, sin_tab, *, out_scale, tm):
    b, l, _ = zs.shape
    tm = min(tm, l)
    return pl.pallas_call(
        functools.partial(_q_kernel, out_scale=out_scale),
        grid=(b, l // tm),
        in_specs=[
            pl.BlockSpec((None, tm, Q_LORA), lambda bi, i: (bi, i, ZS_Q // Q_LORA)),
            pl.BlockSpec((1, Q_LORA), lambda bi, i: (0, 0)),
            pl.BlockSpec(w_q.shape, lambda bi, i: (0, 0)),
            pl.BlockSpec((1, QK_NOPE), lambda bi, i: (0, 0)),
            pl.BlockSpec((1, N_HEADS * QK_ROPE), lambda bi, i: (0, 0)),
            pl.BlockSpec((1, N_HEADS * QK_ROPE), lambda bi, i: (0, 0)),
            pl.BlockSpec((tm, LANES), lambda bi, i: (i, 0)),
            pl.BlockSpec((tm, LANES), lambda bi, i: (i, 0)),
        ],
        out_specs=pl.BlockSpec((None, N_HEADS, tm, HEAD_PAD), lambda bi, i: (bi, 0, i, 0)),
        out_shape=jax.ShapeDtypeStruct((b, N_HEADS, l, HEAD_PAD), BF16),
        compiler_params=_cparams("parallel", "parallel"),
        name="q_proj",
    )(zs, q_lat_g.reshape(1, Q_LORA), w_q, gn, gr, grs, cos_tab, sin_tab)


def _kv_kernel(z_ref, gl_ref, w_ref, gn_ref, gt_ref, tab_ref, k_ref, v_ref):
    z = z_ref[...]
    zkv = z[:, :KV_LORA]
    zn = (zkv * _rms_scale(zkv, KV_LORA) * gl_ref[...]).astype(BF16)
    kv = jnp.dot(zn, w_ref[...], preferred_element_type=F32)
    lo = lax.broadcasted_iota(jnp.int32, (1, LANES), 1) < QK_ROPE
    kr2 = z[:, KV_LORA:KV_LORA + LANES]
    ss_r = jnp.sum(jnp.where(lo, kr2 * kr2, 0.0), axis=-1, keepdims=True)
    t = kr2 * gt_ref[...] * tab_ref[...]
    rot = t + pltpu.roll(t, QK_ROPE, axis=1)
    gn = gn_ref[...]
    ones = jnp.ones((z.shape[0], LANES), v_ref.dtype)
    for h in range(N_HEADS):
        nope = kv[:, h * QK_NOPE:(h + 1) * QK_NOPE]
        inv = lax.rsqrt((jnp.sum(nope * nope, axis=-1, keepdims=True) + ss_r) * (1.0 / QK_HEAD) + EPS)
        k_ref[h, :, 0:LANES] = (nope * inv * gn).astype(k_ref.dtype)
        keep = lo if h % 2 == 0 else jnp.logical_not(lo)
        k_ref[h, :, LANES:HEAD_PAD] = jnp.where(keep, rot * inv, 0.0).astype(k_ref.dtype)
        v_ref[h, :, 0:LANES] = kv[:, N_HEADS * QK_NOPE + h * V_HEAD:N_HEADS * QK_NOPE + (h + 1) * V_HEAD].astype(
            v_ref.dtype)
        v_ref[h, :, LANES:HEAD_PAD] = ones


def _kv_proj(zs, kv_lat_g, w_kv, gn, gt, tab, *, tm):
    b, l, _ = zs.shape
    tm = min(tm, l)
    blk = KV_LORA + 2 * LANES
    shape = jax.ShapeDtypeStruct((b, N_HEADS, l, HEAD_PAD), BF16)
    spec = pl.BlockSpec((None, N_HEADS, tm, HEAD_PAD), lambda bi, i: (bi, 0, i, 0))
    return pl.pallas_call(
        _kv_kernel,
        grid=(b, l // tm),
        in_specs=[
            pl.BlockSpec((None, tm, blk), lambda bi, i: (bi, i, ZS_KV // blk)),
            pl.BlockSpec((1, KV_LORA), lambda bi, i: (0, 0)),
            pl.BlockSpec(w_kv.shape, lambda bi, i: (0, 0)),
            pl.BlockSpec((1, QK_NOPE), lambda bi, i: (0, 0)),
            pl.BlockSpec((1, LANES), lambda bi, i: (0, 0)),
            pl.BlockSpec((tm, LANES), lambda bi, i: (i, 0)),
        ],
        out_specs=[spec, spec],
        out_shape=[shape, shape],
        compiler_params=_cparams("parallel", "parallel"),
        name="kv_proj",
    )(zs, kv_lat_g.reshape(1, KV_LORA), w_kv, gn, gt, tab)


def _attn_kernel(q_ref, *refs, chunks):
    n_src = len(chunks)
    o_ref, acc_ref, m_ref = refs[2 * n_src:]
    q = q_ref[...]
    acc_ref[...] = jnp.zeros_like(acc_ref)
    m_ref[...] = jnp.full_like(m_ref, -jnp.inf)

    def step(k, v):
        s = lax.dot_general(q, k, (((1,), (1,)), ((), ())), preferred_element_type=F32)
        m_prev = m_ref[...]
        m_new = jnp.maximum(m_prev, jnp.max(s, axis=-1, keepdims=True))
        alpha = jnp.exp2(m_prev - m_new)
        p = jnp.exp2(s - jnp.concatenate([m_new] * (s.shape[1] // LANES), axis=1)).astype(BF16)
        pv = jnp.dot(p, v, preferred_element_type=F32)
        acc_ref[...] = acc_ref[...] * jnp.concatenate([alpha, alpha], axis=1) + pv
        m_ref[...] = m_new

    for si, (bk, n) in enumerate(chunks):
        k_ref, v_ref = refs[2 * si], refs[2 * si + 1]
        if n == 1:
            step(k_ref[...], v_ref[...])
        else:
            def body(c, carry, k_ref=k_ref, v_ref=v_ref, bk=bk):
                off = pl.multiple_of(c * bk, bk)
                step(k_ref[pl.ds(off, bk), :], v_ref[pl.ds(off, bk), :])
                return carry

            lax.fori_loop(0, n, body, 0, unroll=ATT_UNROLL)
    acc = acc_ref[...]
    o_ref[...] = (acc[:, :V_HEAD] / acc[:, V_HEAD:]).astype(o_ref.dtype)


def _attention(q, sources, *, bq, bk):
    b, h, l, _ = q.shape
    bq = min(bq, l)
    chunks = []
    in_specs = [pl.BlockSpec((None, None, bq, HEAD_PAD), lambda bi, hi, i: (bi, hi, i, 0))]
    args = [q]
    for k, v in sources:
        lk = k.shape[2]
        cb = min(bk, lk)
        chunks.append((cb, lk // cb))
        for a in (k, v):
            in_specs.append(pl.BlockSpec((None, None, lk, HEAD_PAD), lambda bi, hi, i: (bi, hi, 0, 0)))
            args.append(a)
    return pl.pallas_call(
        functools.partial(_attn_kernel, chunks=tuple(chunks)),
        grid=(b, h, l // bq),
        in_specs=in_specs,
        out_specs=pl.BlockSpec((None, bq, V_HEAD), lambda bi, hi, i: (bi, i, hi)),
        out_shape=jax.ShapeDtypeStruct((b, l, h * V_HEAD), BF16),
        scratch_shapes=[pltpu.VMEM((bq, HEAD_PAD), F32), pltpu.VMEM((bq, LANES), F32)],
        compiler_params=_cparams("parallel", "parallel", "arbitrary"),
        name="latent_attention",
    )(*args)


def _merge_out_kernel(p_ref, a_ref, s_ref, gp_ref, ga_ref, gs_ref, wp_ref, wa_ref, ws_ref, wo_ref, x_ref, g1_ref,
                      o_ref, y_ref, *, nj, tn):
    j = pl.program_id(2)

    @pl.when(j < nj)
    def _():
        y = gp_ref[...].astype(F32) * jnp.dot(p_ref[...], wp_ref[j], preferred_element_type=F32)
        y += ga_ref[...].astype(F32) * jnp.dot(a_ref[...], wa_ref[j], preferred_element_type=F32)
        y += gs_ref[...].astype(F32) * jnp.dot(s_ref[...], ws_ref[j], preferred_element_type=F32)
        y_ref[j] = y.astype(y_ref.dtype)

    @pl.when(j >= nj)
    def _():
        wo = wo_ref.at[j - nj]
        acc = jnp.dot(y_ref[0], wo[0:tn, :], preferred_element_type=F32)
        for c in range(1, nj):
            acc += jnp.dot(y_ref[c], wo[c * tn:(c + 1) * tn, :], preferred_element_type=F32)
        o_ref[...] = x_ref[...] + g1_ref[...] * acc


def _merge_out(pool_out, attn, sg_out, gates, w_p, w_a, w_s, w_o, layer, x, mod, *, tm, tn):
    b, l, _ = pool_out.shape
    nj = w_p.shape[1]
    d = nj * tn
    tm = min(tm, l)

    def first(j):
        return jnp.minimum(j, nj - 1)

    def second(j):
        return jnp.maximum(j - nj, 0)

    def act(width):
        return pl.BlockSpec((None, tm, width), lambda bi, i, j: (bi, i, 0))

    def gate(branch):
        return pl.BlockSpec((None, tm, tn), lambda bi, i, j: (bi, i, branch * nj + first(j)))

    def wgt(width):
        return pl.BlockSpec((None, nj, width, tn), lambda bi, i, j: (layer, 0, 0, 0), pipeline_mode=pl.Buffered(1))

    return pl.pallas_call(
        functools.partial(_merge_out_kernel, nj=nj, tn=tn),
        grid=(b, l // tm, 2 * nj),
        in_specs=[act(POOL_W), act(N_HEADS * V_HEAD), act(SG_W), gate(0), gate(1), gate(2),
                  wgt(POOL_W), wgt(N_HEADS * V_HEAD), wgt(SG_W), wgt(d),
                  pl.BlockSpec((None, tm, tn), lambda bi, i, j: (bi, i, second(j))),
                  pl.BlockSpec((None, 1, tn), lambda bi, i, j: (bi, 0, second(j)))],
        out_specs=pl.BlockSpec((None, tm, tn), lambda bi, i, j: (bi, i, second(j))),
        out_shape=jax.ShapeDtypeStruct((b, l, d), F32),
        scratch_shapes=[pltpu.VMEM((nj, tm, tn), BF16)],
        compiler_params=_cparams("parallel", "parallel", "arbitrary"),
        name="merge_out_proj",
    )(pool_out, attn, sg_out, gates, gates, gates, w_p, w_a, w_s, w_o, x, mod[:, 2:3, :])


def _mm_res_kernel(a_ref, w_ref, x_ref, gate_ref, o_ref):
    y = jnp.dot(a_ref[...], w_ref[...], preferred_element_type=F32)
    o_ref[...] = x_ref[...] + gate_ref[...] * y


def _mm_residual(a, w, layer, x, mod, gate_row, *, tm, tn, name):
    b, l, k = a.shape
    d = w.shape[2]
    tm = min(tm, l)
    return pl.pallas_call(
        _mm_res_kernel,
        grid=(b, l // tm, d // tn),
        in_specs=[
            pl.BlockSpec((None, tm, k), lambda bi, i, j: (bi, i, 0)),
            pl.BlockSpec((None, k, tn), lambda bi, i, j: (layer, 0, j)),
            pl.BlockSpec((None, tm, tn), lambda bi, i, j: (bi, i, j)),
            pl.BlockSpec((None, 1, tn), lambda bi, i, j: (bi, 0, j)),
        ],
        out_specs=pl.BlockSpec((None, tm, tn), lambda bi, i, j: (bi, i, j)),
        out_shape=jax.ShapeDtypeStruct((b, l, d), F32),
        compiler_params=_cparams("parallel", "parallel", "arbitrary"),
        name=name,
    )(a, w, x, mod[:, gate_row:gate_row + 1, :])


def _ffn_up_kernel(h_ref, wa_ref, wv_ref, cwa_ref, cwv_ref, cba_ref, cbv_ref, o_ref, *, tm):
    halo = BF16_ROWS
    h = h_ref[...]
    rows = h.shape[0]

    def conv(w_ref, cw_ref, cb_ref):
        u = jnp.dot(h, w_ref[...], preferred_element_type=F32)
        acc = cb_ref[...] + pltpu.roll(u, 1, axis=0) * cw_ref[0:1, :]
        acc = acc + u * cw_ref[1:2, :]
        acc = acc + pltpu.roll(u, rows - 1, axis=0) * cw_ref[2:3, :]
        return acc[halo:halo + tm, :]

    a = conv(wa_ref, cwa_ref, cba_ref)
    v = conv(wv_ref, cwv_ref, cbv_ref)
    o_ref[...] = (a * _sigmoid(a) * v).astype(o_ref.dtype)


def _ffn_up(h_tiles, w_up, layer, conv_w, conv_b, *, tf):
    b, nt, rows, d = h_tiles.shape
    tm = rows - 2 * BF16_ROWS
    l = nt * tm
    d_ff = w_up.shape[2] // 2
    nf = d_ff // tf
    return pl.pallas_call(
        functools.partial(_ffn_up_kernel, tm=tm),
        grid=(b, nt, nf),
        in_specs=[
            pl.BlockSpec((None, None, rows, d), lambda bi, i, j: (bi, i, 0, 0)),
            pl.BlockSpec((None, d, tf), lambda bi, i, j: (layer, 0, j)),
            pl.BlockSpec((None, d, tf), lambda bi, i, j: (layer, 0, nf + j)),
            pl.BlockSpec((CONV_W, tf), lambda bi, i, j: (0, j)),
            pl.BlockSpec((CONV_W, tf), lambda bi, i, j: (0, nf + j)),
            pl.BlockSpec((1, tf), lambda bi, i, j: (0, j)),
            pl.BlockSpec((1, tf), lambda bi, i, j: (0, nf + j)),
        ],
        out_specs=pl.BlockSpec((None, tm, tf), lambda bi, i, j: (bi, i, j)),
        out_shape=jax.ShapeDtypeStruct((b, l, d_ff), BF16),
        compiler_params=_cparams("parallel", "parallel", "arbitrary"),
        name="ffn_up_conv_gate",
    )(h_tiles, w_up, w_up, conv_w, conv_w, conv_b.reshape(1, -1), conv_b.reshape(1, -1))


def _rope_tables(n_tokens, rotate):
    if rotate:
        t = np.arange(n_tokens)
        inv_freq = ROPE_THETA ** (-np.arange(ROPE_FREQS, dtype=np.float64) / ROPE_FREQS)
        ar = (t // GRID_W)[:, None] * inv_freq
        ac = (t % GRID_W)[:, None] * inv_freq
        c = np.concatenate([np.cos(ar), np.cos(ar), np.cos(ac), np.cos(ac)], axis=1)
        s = np.concatenate([-np.sin(ar), np.sin(ar), -np.sin(ac), np.sin(ac)], axis=1)
    else:
        c, s = np.ones((n_tokens, QK_ROPE)), np.zeros((n_tokens, QK_ROPE))
    cat = lambda a, b_: jnp.asarray(np.concatenate([a, b_], axis=1), F32)
    return dict(q_cos=cat(c, c), q_sin=cat(s, s), k=cat(c, s))


def _stacked_weights(w_in, w_br_pool, w_br_mla, w_br_sg, w_o, ffn_up, ffn_down):
    depth, d, _ = w_in.shape
    o_q, o_kv, o_kr, o_sg, o_gate = POOL_W, POOL_W + Q_LORA, POOL_W + Q_LORA + KV_LORA, \
        POOL_W + Q_LORA + KV_LORA + QK_ROPE, POOL_W + Q_LORA + KV_LORA + QK_ROPE + 2 * SG_W
    w_kr = w_in[:, :, o_kr:o_sg]
    w_small = jnp.concatenate([
        w_in[:, :, :o_q], w_in[:, :, o_q:o_kv], w_in[:, :, o_sg:o_gate], w_in[:, :, o_kv:o_kr], w_kr,
        w_kr[:, :, ROPE_SWAP], jnp.zeros((depth, d, ZS_COLS - ZS_KV - KV_LORA - 2 * QK_ROPE), w_in.dtype)],
        axis=2).astype(BF16)
    def col_tiles(w):
        dep, k, n = w.shape
        return w.astype(BF16).reshape(dep, k, n // MERGE_TN, MERGE_TN).transpose(0, 2, 1, 3)

    return dict(
        w_small=w_small, w_gate=w_in[:, :, o_gate:].astype(BF16), w_br_pool=col_tiles(w_br_pool),
        w_br_mla=col_tiles(w_br_mla), w_br_sg=col_tiles(w_br_sg), w_o=col_tiles(w_o),
        ffn_up=ffn_up.astype(BF16), ffn_down=ffn_down.astype(BF16))


def _layer_weights(p):
    w_uq = p["w_uq"].reshape(Q_LORA, N_HEADS, QK_HEAD)
    w_q_rope = w_uq[:, :, QK_NOPE:]
    w_q = jnp.concatenate([
        w_uq[:, :, :QK_NOPE].reshape(Q_LORA, -1), w_q_rope.reshape(Q_LORA, -1),
        w_q_rope[:, :, ROPE_SWAP].reshape(Q_LORA, -1)], axis=1).astype(BF16)
    gq = p["q_norm_g"]
    gq_n = gq[:QK_NOPE].reshape(1, QK_NOPE)
    gq_r = jnp.tile(gq[QK_NOPE:], N_HEADS).reshape(1, -1)
    gq_rs = jnp.tile(gq[QK_NOPE:][ROPE_SWAP], N_HEADS).reshape(1, -1)

    w_ukv = p["w_ukv"].reshape(KV_LORA, N_HEADS, QK_NOPE + V_HEAD)
    w_kv = jnp.concatenate([
        w_ukv[:, :, :QK_NOPE].reshape(KV_LORA, -1), w_ukv[:, :, QK_NOPE:].reshape(KV_LORA, -1)], axis=1).astype(BF16)
    gk = p["k_norm_g"]
    gk_n = gk[:QK_NOPE].reshape(1, QK_NOPE)
    gk_t = jnp.concatenate([gk[QK_NOPE:], gk[QK_NOPE:][ROPE_SWAP]]).reshape(1, LANES)
    return dict(
        w_q=w_q, gq_n=gq_n, gq_r=gq_r, gq_rs=gq_rs, w_kv=w_kv, gk_n=gk_n, gk_t=gk_t,
        pool_w=p["pool_w"].astype(BF16), sg_w=p["sg_w"].astype(BF16))


def _keys_values(zs, p, w, tabs):
    return _kv_proj(zs, p["kv_lat_g"], w["w_kv"], w["gk_n"], w["gk_t"], tabs["k"], tm=MIXER_TM)


def _queries(zs, p, w, tabs):
    out_scale = QK_HEAD ** -0.5 * math.log2(math.e)
    return _q_proj(zs, p["q_lat_g"], w["w_q"], w["gq_n"], w["gq_r"], w["gq_rs"], tabs["q_cos"], tabs["q_sin"],
                   out_scale=out_scale, tm=MIXER_TM)


def _in_proj(x, p, ws, layer, mod, *, gates):
    h = _norm_modulate(x, p["norm1_g"], mod, shift_row=0, scale_row=1, tm=DENSE_TM)
    zs = _matmul(h, ws["w_small"], layer, sigmoid=False, out_dtype=F32, tm=DENSE_TM, tn=SMALL_TN, name="in_proj_small")
    g = None
    if gates:
        g = _matmul(h, ws["w_gate"], layer, sigmoid=True, out_dtype=BF16, tm=DENSE_TM, tn=GATE_TN,
                    name="in_proj_gate")
    return zs, g


def _mix_and_ffn(x, zs, gates, attn, p, w, ws, layer, mod):
    pool_out = _pool_mix(zs, w["pool_w"], p["pool_scale"], tm=MIXER_TM)
    sg_out = _spatial_gating(zs, p["sg_norm_g"], w["sg_w"], p["sg_b"], tm=MIXER_TM)
    x = _merge_out(pool_out, attn, sg_out, gates, ws["w_br_pool"], ws["w_br_mla"], ws["w_br_sg"], ws["w_o"], layer,
                   x, mod, tm=DENSE_TM, tn=MERGE_TN)
    h2 = _norm_modulate_tiles(x, p["norm2_g"], mod, shift_row=3, scale_row=4, tm=DENSE_TM)
    act = _ffn_up(h2, ws["ffn_up"], layer, p["ffn_conv_w"], p["ffn_conv_b"], tf=FFN_TF)
    return _mm_residual(act, ws["ffn_down"], layer, x, mod, 5, tm=DENSE_TM, tn=DOWN_TN, name="ffn_down_residual")


def kernel(x, c, ctx, c_ctx, ada_w, ada_b, norm1_g, w_in, pool_w, pool_scale, q_lat_g, w_uq, kv_lat_g, w_ukv, q_norm_g, k_norm_g, sg_norm_g, sg_w, sg_b, w_br_pool, w_br_mla, w_br_sg, w_o, norm2_g, ffn_up, ffn_conv_w, ffn_conv_b, ffn_down):
    b, l, d = x.shape
    lc = ctx.shape[1]
    depth = ada_w.shape[0]
    assert b + 1 <= SUBLANES

    cvec = jnp.concatenate([c, c_ctx[None, :], jnp.zeros((SUBLANES - b - 1, d), c.dtype)], axis=0)
    mod = _ada(cvec, ada_w, ada_b).reshape(depth, SUBLANES, 6, d)
    rope_lat = _rope_tables(l, rotate=True)
    rope_ctx = _rope_tables(lc, rotate=False)
    ws = _stacked_weights(w_in, w_br_pool, w_br_mla, w_br_sg, w_o, ffn_up, ffn_down)

    xl, xc = x, ctx
    for i in range(depth):
        p = {
            "norm1_g": norm1_g[i], "pool_w": pool_w[i], "pool_scale": pool_scale[i],
            "q_lat_g": q_lat_g[i], "w_uq": w_uq[i], "kv_lat_g": kv_lat_g[i], "w_ukv": w_ukv[i],
            "q_norm_g": q_norm_g[i], "k_norm_g": k_norm_g[i], "sg_norm_g": sg_norm_g[i], "sg_w": sg_w[i],
            "sg_b": sg_b[i], "norm2_g": norm2_g[i], "ffn_conv_w": ffn_conv_w[i], "ffn_conv_b": ffn_conv_b[i],
        }
        w = _layer_weights(p)
        mod_lat = mod[i, :b]
        mod_ctx = jnp.broadcast_to(mod[i, b:b + 1], (b, 6, d))

        zl, gl = _in_proj(xl, p, ws, i, mod_lat, gates=True)
        zc, gc = _in_proj(xc, p, ws, i, mod_ctx, gates=i < depth - 1)
        kc, vc = _keys_values(zc, p, w, rope_ctx)
        kl, vl = _keys_values(zl, p, w, rope_lat)
        ql = _queries(zl, p, w, rope_lat)
        attn_l = _attention(ql, [(kl, vl), (kc, vc)], bq=ATT_BQ, bk=ATT_BK)
        xl = _mix_and_ffn(xl, zl, gl, attn_l, p, w, ws, i, mod_lat)
        if i < depth - 1:
            qc = _queries(zc, p, w, rope_ctx)
            attn_c = _attention(qc, [(kc, vc)], bq=ATT_BQ, bk=ATT_BK)
            xc = _mix_and_ffn(xc, zc, gc, attn_c, p, w, ws, i, mod_ctx)
    return xl
```
